```python
import math
import jax, jax.numpy as jnp
from jax import lax
import numpy as np

D_MODEL = 1024
BATCH = 16
SEQ = 4096
DEPTH = 2
DEC_BATCH = 16
DEC_SEQ = 16
PAST_LEN = 2048

CHUNK = 64
N_META = 16
N_EVEN = (DEPTH + 1) // 2
N_ODD = DEPTH // 2
EPS = 1e-6
CONV_K = 4
S5_GROUP = 16
S5_STATE = 64
W_A = D_MODEL // 2
G_A = W_A // S5_GROUP
W_B = D_MODEL
H_B = 4
DH_B = W_B // H_B
W_C = D_MODEL
H_C = 4
DV_C = W_C // H_C
DK_C = DV_C // 2
QK_C = H_C * DK_C
ROPE_BASE = 10000.0
W_D = D_MODEL
H_D = 8
BD_D = W_D // H_D
LRU_C = 8.0
EV_IN = 2 * W_A + 2 * W_B
EV_MIX = W_A + W_B
OD_IN = 2 * QK_C + 2 * W_C + 2 * W_D
OD_MIX = W_C + W_D
F32 = jnp.float32

kernel_name = 'hybrid_s5_mlstm_retention_rglru_stream_step'


def rmsnorm(x, g):
    xf = x.astype(F32)
    y = xf * lax.rsqrt(jnp.mean(xf * xf, axis=-1, keepdims=True) + EPS)
    return (y * g.astype(F32)).astype(x.dtype)


def head_norm(h, g):
    mu = jnp.mean(h, axis=-1, keepdims=True)
    var = jnp.mean(jnp.square(h - mu), axis=-1, keepdims=True)
    y = (h - mu) * lax.rsqrt(var + EPS)
    b, t, nh, d = h.shape
    return y.reshape(b, t, nh * d) * g.astype(F32)


def causal_conv(x, buf, w, b):
    L = x.shape[1]
    xp = jnp.concatenate([buf.astype(x.dtype), x], axis=1)
    out = b.astype(F32)
    for tap in range(CONV_K):
        out = out + xp[:, tap:tap + L].astype(F32) * w[tap].astype(F32)
    return out, xp[:, L:]


def _real_combine(e1, e2):
    a1, b1 = e1
    a2, b2 = e2
    return a1 * a2, a2 * b1 + b2


def _cplx_combine(e1, e2):
    ar1, ai1, br1, bi1 = e1
    ar2, ai2, br2, bi2 = e2
    return (ar1 * ar2 - ai1 * ai2, ar1 * ai2 + ai1 * ar2,
            ar2 * br1 - ai2 * bi1 + br2, ar2 * bi1 + ai2 * br1 + bi2)


def to_chunks(a, c):
    b, L, nh = a.shape[:3]
    a = a.reshape((b, L // c, c, nh) + a.shape[3:])
    return jnp.swapaxes(jnp.moveaxis(a, 1, 0), 2, 3)


def from_chunks(o):
    o = jnp.moveaxis(jnp.swapaxes(o, 2, 3), 0, 1)
    b, n, c, nh, d = o.shape
    return o.reshape(b, n * c, nh, d)


def run_chunked(step, seqs, state, segments):
    outs = []
    start = 0
    for length, c in segments:
        xs = tuple(to_chunks(s[:, start:start + length], c) for s in seqs)
        state, o = lax.scan(step, state, xs)
        outs.append(from_chunks(o))
        start += length
    return jnp.concatenate(outs, axis=1), state


def s5_mixer(u, h0_re, h0_im, lam_re, lam_im, log_dt, b_re, b_im, c_re, c_im, d, w_glu, b_glu):
    bsz, T, _ = u.shape
    uf = u.astype(F32)
    ug = uf.reshape(bsz, T, G_A, S5_GROUP)
    lam_re = lam_re.astype(F32)
    lam_im = lam_im.astype(F32)
    dt = jnp.exp(log_dt.astype(F32))[:, None]
    mag = jnp.exp(lam_re * dt)
    ab_re = mag * jnp.cos(lam_im * dt)
    ab_im = mag * jnp.sin(lam_im * dt)
    den = lam_re * lam_re + lam_im * lam_im
    nr = ab_re - 1.0
    k_re = (nr * lam_re + ab_im * lam_im) / den
    k_im = (ab_im * lam_re - nr * lam_im) / den
    b_re = b_re.astype(F32)
    b_im = b_im.astype(F32)
    bb_re = k_re[..., None] * b_re - k_im[..., None] * b_im
    bb_im = k_re[..., None] * b_im + k_im[..., None] * b_re
    bu_re = jnp.einsum('btgc,gpc->btgp', ug, bb_re)
    bu_im = jnp.einsum('btgc,gpc->btgp', ug, bb_im)
    h0_re = h0_re.astype(F32)
    h0_im = h0_im.astype(F32)
    bu_re = bu_re.at[:, 0].add(ab_re * h0_re - ab_im * h0_im)
    bu_im = bu_im.at[:, 0].add(ab_re * h0_im + ab_im * h0_re)
    a_re = jnp.broadcast_to(ab_re[None, None], (1, T, G_A, S5_STATE))
    a_im = jnp.broadcast_to(ab_im[None, None], (1, T, G_A, S5_STATE))
    _, _, h_re, h_im = lax.associative_scan(_cplx_combine, (a_re, a_im, bu_re, bu_im), axis=1)
    y = (jnp.einsum('btgp,gcp->btgc', h_re, c_re.astype(F32))
         - jnp.einsum('btgp,gcp->btgc', h_im, c_im.astype(F32)))
    y = y.reshape(bsz, T, W_A) + d.astype(F32) * uf
    y = jax.nn.gelu(y)
    y = y * jax.nn.sigmoid(y @ w_glu.astype(F32) + b_glu.astype(F32))
    return y, h_re[:, -1], h_im[:, -1]


def mlstm_step(carry, xs):
    C0, n0, m0 = carry
    q, k, v, ig, lf = xs
    c = q.shape[2]
    b = jnp.cumsum(lf, axis=-1)
    causal = jnp.tril(jnp.ones((c, c), dtype=bool))
    dlog = jnp.where(causal, b[..., :, None] - b[..., None, :] + ig[..., None, :], -jnp.inf)
    inter = b + m0[..., None]
    m = jnp.maximum(inter, jnp.max(dlog, axis=-1))
    s = jnp.einsum('bhtd,bhsd->bhts', q, k) * jnp.exp(dlog - m[..., None])
    w_inter = jnp.exp(inter - m)
    num = jnp.einsum('bhts,bhse->bhte', s, v) + w_inter[..., None] * jnp.einsum('bhtd,bhde->bhte', q, C0)
    den = jnp.sum(s, axis=-1) + w_inter * jnp.einsum('bhtd,bhd->bht', q, n0)
    h = num / jnp.maximum(jnp.abs(den), jnp.exp(-m))[..., None]
    m_end = m[..., -1]
    w_end = jnp.exp(b[..., -1:] - b + ig - m_end[..., None])
    decay = jnp.exp(inter[..., -1] - m_end)
    kw = k * w_end[..., None]
    C1 = decay[..., None, None] * C0 + jnp.einsum('bhsd,bhse->bhde', kw, v)
    n1 = decay[..., None] * n0 + jnp.sum(kw, axis=2)
    return (C1, n1, m_end), h


def mlstm_mixer(x, conv_buf, C0, n0, m0, conv_w, conv_b, wq, wk, wv, w_if, b_if, norm_w, skip, segments):
    bsz, T, _ = x.shape
    xc, new_buf = causal_conv(x, conv_buf, conv_w, conv_b)
    xc = jax.nn.silu(xc)
    xh = x.astype(F32).reshape(bsz, T, H_B, DH_B)
    ch = xc.reshape(bsz, T, H_B, DH_B)
    q = jnp.einsum('bthd,hde->bthe', ch, wq.astype(F32))
    k = jnp.einsum('bthd,hde->bthe', ch, wk.astype(F32))
    v = jnp.einsum('bthd,hde->bthe', xh, wv.astype(F32))
    qkv = jnp.concatenate([q.reshape(bsz, T, W_B), k.reshape(bsz, T, W_B), v.reshape(bsz, T, W_B)], axis=-1)
    gates = qkv @ w_if.astype(F32) + b_if.astype(F32)
    ig = gates[..., :H_B]
    lf = jax.nn.log_sigmoid(gates[..., H_B:])
    k = k * (DH_B ** -0.5)
    h, (C1, n1, m1) = run_chunked(mlstm_step, (q, k, v, ig, lf),
                                  (C0.astype(F32), n0.astype(F32), m0.astype(F32)), segments)
    y = head_norm(h, norm_w) + skip.astype(F32) * xc
    return y, C1, n1, m1, new_buf


def retention_log_decay():
    return jnp.log1p(-jnp.exp2(-5.0 - jnp.arange(H_C, dtype=F32)))


def retention_step(S, xs):
    q, k, v = xs
    c = q.shape[2]
    log_g = retention_log_decay()
    idx = jnp.arange(c, dtype=F32)
    diff = idx[:, None] - idx[None, :]
    dmask = jnp.where(diff >= 0, jnp.exp(log_g[:, None, None] * jnp.maximum(diff, 0.0)), 0.0)
    inner = jnp.einsum('bhtd,bhsd->bhts', q, k) * dmask
    xi = jnp.exp(log_g[:, None] * (idx + 1.0))[..., None]
    o = jnp.einsum('bhts,bhse->bhte', inner, v) + xi * jnp.einsum('bhtd,bhde->bhte', q, S)
    zeta = jnp.exp(log_g[:, None] * (c - 1.0 - idx))[..., None]
    S1 = jnp.exp(log_g * c)[:, None, None] * S + jnp.einsum('bhsd,bhse->bhde', k * zeta, v)
    return S1, o


def rope(x, pos):
    half = x.shape[-1] // 2
    inv = ROPE_BASE ** (-jnp.arange(half, dtype=F32) / half)
    ang = pos[:, None] * inv[None, :]
    cos = jnp.cos(ang)[None, :, None, :]
    sin = jnp.sin(ang)[None, :, None, :]
    x1, x2 = x[..., :half], x[..., half:]
    return jnp.concatenate([x1 * cos - x2 * sin, x1 * sin + x2 * cos], axis=-1)


def retention_mixer(q, k, v, S0, pos, norm_w, segments):
    q = rope(q.astype(F32), pos)
    k = rope(k.astype(F32), pos) * (DK_C ** -0.5)
    o, S1 = run_chunked(retention_step, (q, k, v.astype(F32)), S0.astype(F32), segments)
    return head_norm(o, norm_w), S1


def rglru_mixer(x, conv_buf, h0, conv_w, conv_b, w_a, b_a, w_x, b_x, lam):
    bsz, T, _ = x.shape
    xc, new_buf = causal_conv(x, conv_buf, conv_w, conv_b)
    xb = xc.reshape(bsz, T, H_D, BD_D)
    r = jax.nn.sigmoid(jnp.einsum('bthd,hde->bthe', xb, w_a.astype(F32)).reshape(bsz, T, W_D) + b_a.astype(F32))
    i = jax.nn.sigmoid(jnp.einsum('bthd,hde->bthe', xb, w_x.astype(F32)).reshape(bsz, T, W_D) + b_x.astype(F32))
    log_a = -LRU_C * r * jax.nn.softplus(-lam.astype(F32))
    a = jnp.exp(log_a)
    bx = jnp.sqrt(-jnp.expm1(2.0 * log_a)) * (i * xc)
    bx = bx.at[:, 0].add(a[:, 0] * h0.astype(F32))
    _, h = lax.associative_scan(_real_combine, (a, bx), axis=1)
    return h, h[:, -1], new_buf


def even_layer(h, g, i, st, P, segments):
    u = rmsnorm(h, g)
    ua, za, xb, zb = jnp.split(u @ P['ev_w_in'][i], [W_A, 2 * W_A, 2 * W_A + W_B], axis=-1)
    ya, s_re, s_im = s5_mixer(ua, st['s5_re'][i], st['s5_im'][i], P['s5_lambda_re'][i], P['s5_lambda_im'][i],
                              P['s5_log_dt'][i], P['s5_b_re'][i], P['s5_b_im'][i], P['s5_c_re'][i],
                              P['s5_c_im'][i], P['s5_d'][i], P['s5_w_glu'][i], P['s5_b_glu'][i])
    yb, mc, mn, mm, mbuf = mlstm_mixer(xb, st['ml_conv'][i], st['ml_c'][i], st['ml_n'][i], st['ml_m'][i],
                                       P['ml_conv_w'][i], P['ml_conv_b'][i], P['ml_wq'][i], P['ml_wk'][i],
                                       P['ml_wv'][i], P['ml_w_if'][i], P['ml_b_if'][i], P['ml_norm_w'][i],
                                       P['ml_skip'][i], segments)
    gated = jnp.concatenate([ya * jax.nn.silu(za.astype(F32)), yb * jax.nn.silu(zb.astype(F32))], axis=-1)
    h = h + (gated.astype(h.dtype) @ P['ev_w_out'][i]).astype(h.dtype)
    return h, dict(s5_re=s_re, s5_im=s_im, ml_c=mc, ml_n=mn, ml_m=mm, ml_conv=mbuf)


def odd_layer(h, g, i, st, P, pos, segments):
    u = rmsnorm(h, g)
    bsz, T, _ = u.shape
    cuts = [QK_C, 2 * QK_C, 2 * QK_C + W_C, 2 * QK_C + 2 * W_C, 2 * QK_C + 2 * W_C + W_D]
    qc, kc, vc, zc, xd, zd = jnp.split(u @ P['od_w_in'][i], cuts, axis=-1)
    yc, S1 = retention_mixer(qc.reshape(bsz, T, H_C, DK_C), kc.reshape(bsz, T, H_C, DK_C),
                             vc.reshape(bsz, T, H_C, DV_C), st['ret'][i], pos, P['ret_norm_w'][i], segments)
    yd, hd, dbuf = rglru_mixer(xd, st['lru_conv'][i], st['lru_h'][i], P['lru_conv_w'][i], P['lru_conv_b'][i],
                               P['lru_w_a'][i], P['lru_b_a'][i], P['lru_w_x'][i], P['lru_b_x'][i],
                               P['lru_lambda'][i])
    gated = jnp.concatenate([yc * jax.nn.silu(zc.astype(F32)), yd * jax.nn.silu(zd.astype(F32))], axis=-1)
    h = h + (gated.astype(h.dtype) @ P['od_w_out'][i]).astype(h.dtype)
    return h, dict(ret=S1, lru_h=hd, lru_conv=dbuf)


def run_trunk(h, pos, segments, st, P):
    new = {}
    for layer in range(DEPTH):
        i = layer // 2
        g = P['norm_w'][layer]
        if layer % 2 == 0:
            h, ns = even_layer(h, g, i, st, P, segments)
        else:
            h, ns = odd_layer(h, g, i, st, P, pos, segments)
        for name, val in ns.items():
            new.setdefault(name, []).append(val)
    stacked = {name: jnp.stack(vals) for name, vals in new.items()}
    return rmsnorm(h, P['final_norm_w']), stacked


def zero_states(bsz, act_dtype):
    return dict(
        s5_re=jnp.zeros((N_EVEN, bsz, G_A, S5_STATE), F32),
        s5_im=jnp.zeros((N_EVEN, bsz, G_A, S5_STATE), F32),
        ml_c=jnp.zeros((N_EVEN, bsz, H_B, DH_B, DH_B), F32),
        ml_n=jnp.zeros((N_EVEN, bsz, H_B, DH_B), F32),
        ml_m=jnp.zeros((N_EVEN, bsz, H_B), F32),
        ml_conv=jnp.zeros((N_EVEN, bsz, CONV_K - 1, W_B), act_dtype),
        ret=jnp.zeros((N_ODD, bsz, H_C, DK_C, DV_C), F32),
        lru_h=jnp.zeros((N_ODD, bsz, W_D), F32),
        lru_conv=jnp.zeros((N_ODD, bsz, CONV_K - 1, W_D), act_dtype),
    )


def setup_inputs(seed: int = 0) -> dict:
    key = jax.random.key(seed)
    ks = iter(jax.random.split(key, 64))

    def nrm(shape, scale):
        return scale * jax.random.normal(next(ks), shape, F32)

    def unif(shape, lo, hi):
        return jax.random.uniform(next(ks), shape, F32, lo, hi)

    NE, NO = N_EVEN, N_ODD
    lru_s = unif((NO, W_D), 0.9, 0.999) ** (1.0 / LRU_C)
    return {
        'x_prompt': nrm((BATCH, SEQ, D_MODEL), 1.0),
        'x_sample': nrm((DEC_BATCH, DEC_SEQ, D_MODEL), 1.0),
        'state_s5_re': nrm((NE, DEC_BATCH, G_A, S5_STATE), 0.5),
        'state_s5_im': nrm((NE, DEC_BATCH, G_A, S5_STATE), 0.5),
        'state_ml_c': nrm((NE, DEC_BATCH, H_B, DH_B, DH_B), 0.1),
        'state_ml_n': nrm((NE, DEC_BATCH, H_B, DH_B), 0.1),
        'state_ml_m': nrm((NE, DEC_BATCH, H_B), 1.0),
        'state_ml_conv': nrm((NE, DEC_BATCH, CONV_K - 1, W_B), 1.0),
        'state_ret': nrm((NO, DEC_BATCH, H_C, DK_C, DV_C), 0.3),
        'state_lru_h': nrm((NO, DEC_BATCH, W_D), 0.5),
        'state_lru_conv': nrm((NO, DEC_BATCH, CONV_K - 1, W_D), 1.0),
        'meta': nrm((N_META, D_MODEL), 1.0),
        'norm_w': 1.0 + nrm((DEPTH, D_MODEL), 0.01),
        'final_norm_w': 1.0 + nrm((D_MODEL,), 0.01),
        'ev_w_in': nrm((NE, D_MODEL, EV_IN), D_MODEL ** -0.5),
        'ev_w_out': nrm((NE, EV_MIX, D_MODEL), EV_MIX ** -0.5),
        's5_lambda_re': -0.5 + nrm((NE, G_A, S5_STATE), 0.01),
        's5_lambda_im': jnp.pi * jnp.arange(S5_STATE, dtype=F32) + nrm((NE, G_A, S5_STATE), 0.01),
        's5_log_dt': unif((NE, G_A), math.log(1e-3), math.log(1e-1)),
        's5_b_re': nrm((NE, G_A, S5_STATE, S5_GROUP), (2 * S5_GROUP) ** -0.5),
        's5_b_im': nrm((NE, G_A, S5_STATE, S5_GROUP), (2 * S5_GROUP) ** -0.5),
        's5_c_re': nrm((NE, G_A, S5_GROUP, S5_STATE), S5_STATE ** -0.5),
        's5_c_im': nrm((NE, G_A, S5_GROUP, S5_STATE), S5_STATE ** -0.5),
        's5_d': nrm((NE, W_A), 1.0),
        's5_w_glu': nrm((NE, W_A, W_A), W_A ** -0.5),
        's5_b_glu': nrm((NE, W_A), 0.01),
        'ml_conv_w': nrm((NE, CONV_K, W_B), CONV_K ** -0.5),
        'ml_conv_b': nrm((NE, W_B), 0.01),
        'ml_wq': nrm((NE, H_B, DH_B, DH_B), DH_B ** -0.5),
        'ml_wk': nrm((NE, H_B, DH_B, DH_B), DH_B ** -0.5),
        'ml_wv': nrm((NE, H_B, DH_B, DH_B), DH_B ** -0.5),
        'ml_w_if': nrm((NE, 3 * W_B, 2 * H_B), 0.1 * (3 * W_B) ** -0.5),
        'ml_b_if': jnp.concatenate([nrm((NE, H_B), 0.1),
                                    jnp.linspace(3.0, 6.0, H_B, dtype=F32) + nrm((NE, H_B), 0.01)], axis=-1),
        'ml_norm_w': 1.0 + nrm((NE, W_B), 0.01),
        'ml_skip': 1.0 + nrm((NE, W_B), 0.01),
        'od_w_in': nrm((NO, D_MODEL, OD_IN), D_MODEL ** -0.5),
        'od_w_out': nrm((NO, OD_MIX, D_MODEL), OD_MIX ** -0.5),
        'ret_norm_w': 1.0 + nrm((NO, W_C), 0.01),
        'lru_conv_w': nrm((NO, CONV_K, W_D), CONV_K ** -0.5),
        'lru_conv_b': nrm((NO, W_D), 0.01),
        'lru_w_a': nrm((NO, H_D, BD_D, BD_D), BD_D ** -0.5),
        'lru_b_a': nrm((NO, W_D), 0.1),
        'lru_w_x': nrm((NO, H_D, BD_D, BD_D), BD_D ** -0.5),
        'lru_b_x': nrm((NO, W_D), 0.1),
        'lru_lambda': jnp.log(lru_s) - jnp.log1p(-lru_s),
    }


def reference(x_prompt, x_sample, state_s5_re, state_s5_im, state_ml_c, state_ml_n, state_ml_m, state_ml_conv,
              state_ret, state_lru_h, state_lru_conv, meta, norm_w, final_norm_w, ev_w_in, ev_w_out,
              s5_lambda_re, s5_lambda_im, s5_log_dt, s5_b_re, s5_b_im, s5_c_re, s5_c_im, s5_d, s5_w_glu, s5_b_glu,
              ml_conv_w, ml_conv_b, ml_wq, ml_wk, ml_wv, ml_w_if, ml_b_if, ml_norm_w, ml_skip,
              od_w_in, od_w_out, ret_norm_w, lru_conv_w, lru_conv_b, lru_w_a, lru_b_a, lru_w_x, lru_b_x,
              lru_lambda):
    P = dict(norm_w=norm_w, final_norm_w=final_norm_w, ev_w_in=ev_w_in, ev_w_out=ev_w_out,
             s5_lambda_re=s5_lambda_re, s5_lambda_im=s5_lambda_im, s5_log_dt=s5_log_dt,
             s5_b_re=s5_b_re, s5_b_im=s5_b_im, s5_c_re=s5_c_re, s5_c_im=s5_c_im, s5_d=s5_d,
             s5_w_glu=s5_w_glu, s5_b_glu=s5_b_glu, ml_conv_w=ml_conv_w, ml_conv_b=ml_conv_b,
             ml_wq=ml_wq, ml_wk=ml_wk, ml_wv=ml_wv, ml_w_if=ml_w_if, ml_b_if=ml_b_if,
             ml_norm_w=ml_norm_w, ml_skip=ml_skip, od_w_in=od_w_in, od_w_out=od_w_out,
             ret_norm_w=ret_norm_w, lru_conv_w=lru_conv_w, lru_conv_b=lru_conv_b,
             lru_w_a=lru_w_a, lru_b_a=lru_b_a, lru_w_x=lru_w_x, lru_b_x=lru_b_x, lru_lambda=lru_lambda)

    bp, tp, _ = x_prompt.shape
    meta_b = jnp.broadcast_to(meta.astype(x_prompt.dtype)[None], (bp, N_META, D_MODEL))
    h_p = jnp.concatenate([meta_b, x_prompt], axis=1)
    pos_p = jnp.arange(N_META + tp, dtype=F32)
    seg_p = ((N_META, N_META), (tp, CHUNK))
    out_p, new_p = run_trunk(h_p, pos_p, seg_p, zero_states(bp, x_prompt.dtype), P)
    y_prompt = out_p[:, N_META:]

    ts = x_sample.shape[1]
    pos_s = (N_META + PAST_LEN) + jnp.arange(ts, dtype=F32)
    st_s = dict(s5_re=state_s5_re, s5_im=state_s5_im, ml_c=state_ml_c, ml_n=state_ml_n, ml_m=state_ml_m,
                ml_conv=state_ml_conv, ret=state_ret, lru_h=state_lru_h, lru_conv=state_lru_conv)
    y_sample, new_s = run_trunk(x_sample, pos_s, ((ts, ts),), st_s, P)

    return (y_prompt, y_sample,
            new_p['s5_re'], new_p['s5_im'], new_p['ml_c'], new_p['ml_n'], new_p['ml_m'], new_p['ml_conv'],
            new_p['ret'], new_p['lru_h'], new_p['lru_conv'],
            new_s['s5_re'], new_s['s5_im'], new_s['ml_c'], new_s['ml_n'], new_s['ml_m'], new_s['ml_conv'],
            new_s['ret'], new_s['lru_h'], new_s['lru_conv'])
```

```python
import functools
import math

import numpy as np
import jax
import jax.numpy as jnp
from jax import lax
from jax.experimental import pallas as pl
from jax.experimental.pallas import tpu as pltpu

F32 = jnp.float32
BF16 = jnp.bfloat16

SUBLANES = 8
LANES = 128
MXU_DIM = 256
VMEM_LIMIT = 56 * 1024 * 1024

D_MODEL = 1024
CHUNK = 64
N_META = 16
PAST_LEN = 2048
EPS = 1e-6
CONV_K = 4
S5_GROUP = 16
S5_STATE = 64
W_A = D_MODEL // 2
G_A = W_A // S5_GROUP
W_S = G_A * S5_STATE
W_B = D_MODEL
H_B = 4
DH_B = W_B // H_B
W_C = D_MODEL
H_C = 4
DV_C = W_C // H_C
DK_C = DV_C // 2
QK_C = H_C * DK_C
ROPE_BASE = 10000.0
W_D = D_MODEL
H_D = 8
BD_D = W_D // H_D
LRU_C = 8.0
EV_IN = 2 * W_A + 2 * W_B
OD_IN = 2 * QK_C + 2 * W_C + 2 * W_D
HIST = (CONV_K - 1) * SUBLANES

ROW_TILE = 512
S5_COLS = 512


def _cparams(sem):
    return pltpu.CompilerParams(dimension_semantics=sem, vmem_limit_bytes=VMEM_LIMIT)


def _const_spec(shape):
    nd = len(shape)
    return pl.BlockSpec(shape, lambda *_: (0,) * nd, pipeline_mode=pl.Buffered(1))


def _dot(a, b):
    return jnp.dot(a, b, preferred_element_type=F32)


def _dot_nt(a, b):
    return lax.dot_general(a, b, (((1,), (1,)), ((), ())), preferred_element_type=F32)


def _dot_tn(a, b):
    return lax.dot_general(a, b, (((0,), (0,)), ((), ())), preferred_element_type=F32)


def _rms(x, g):
    return x * lax.rsqrt(jnp.mean(x * x, axis=-1, keepdims=True) + EPS) * g


def _softplus(x):
    return jnp.maximum(x, 0.0) + jnp.log1p(jnp.exp(-jnp.abs(x)))


def _put_cols(dst3, col0, val):
    for j in range(val.shape[1] // LANES):
        dst3[col0 // LANES + j] = val[:, j * LANES:(j + 1) * LANES]


def _get_all(src3):
    return jnp.concatenate([src3[j] for j in range(src3.shape[0])], axis=-1)


def _get_batch(src3, b, tc, col0, width):
    parts = [src3[col0 // LANES + j, pl.ds(b, tc, stride=SUBLANES), :] for j in range(width // LANES)]
    return parts[0] if len(parts) == 1 else jnp.concatenate(parts, axis=-1)


def _put_batch(dst3, b, tc, col0, val):
    for j in range(val.shape[1] // LANES):
        dst3[col0 // LANES + j, pl.ds(b, tc, stride=SUBLANES), :] = val[:, j * LANES:(j + 1) * LANES]


def _head_norm(h):
    mu = jnp.mean(h, axis=-1, keepdims=True)
    d = h - mu
    var = jnp.mean(d * d, axis=-1, keepdims=True)
    return d * lax.rsqrt(var + EPS)


def _proj_in_kernel(x_ref, g_ref, w_ref, o_ref):
    y = _rms(x_ref[...], g_ref[...])
    o_ref[...] = _dot(y.astype(BF16), w_ref[...])


def proj_in(x2, g, w):
    rows, d = x2.shape
    n = w.shape[1]
    tm = min(ROW_TILE, rows)
    return pl.pallas_call(
        _proj_in_kernel,
        grid=(rows // tm,),
        in_specs=[pl.BlockSpec((tm, d), lambda i: (i, 0)),
                  _const_spec((1, d)),
                  _const_spec((d, n))],
        out_specs=pl.BlockSpec((tm, n), lambda i: (i, 0)),
        out_shape=jax.ShapeDtypeStruct((rows, n), F32),
        compiler_params=_cparams(("parallel",)),
        name="proj_in",
    )(x2, g, w)


def _proj_out_kernel(h_ref, a_ref, b_ref, wa_ref, wb_ref, g_ref, o_ref, *, final):
    y = h_ref[...] + _dot(a_ref[...].astype(BF16), wa_ref[...]) + _dot(b_ref[...].astype(BF16), wb_ref[...])
    if final:
        y = _rms(y, g_ref[...])
    o_ref[...] = y


def proj_out(h2, a2, b2, wa, wb, g, final):
    rows, d = h2.shape
    ka, kb = a2.shape[1], b2.shape[1]
    tm = min(ROW_TILE, rows)
    return pl.pallas_call(
        functools.partial(_proj_out_kernel, final=final),
        grid=(rows // tm,),
        in_specs=[pl.BlockSpec((tm, d), lambda i: (i, 0)),
                  pl.BlockSpec((tm, ka), lambda i: (i, 0)),
                  pl.BlockSpec((tm, kb), lambda i: (i, 0)),
                  _const_spec((ka, d)),
                  _const_spec((kb, d)),
                  _const_spec((1, d))],
        out_specs=pl.BlockSpec((tm, d), lambda i: (i, 0)),
        out_shape=jax.ShapeDtypeStruct((rows, d), F32),
        compiler_params=_cparams(("parallel",)),
        name="proj_out",
    )(h2, a2, b2, wa, wb, g)


def _s5_prep_kernel(lre_ref, lim_ref, ldt_ref, bre_ref, bim_ref, are_ref, aim_ref, bbre_ref, bbim_ref):
    lre, lim = lre_ref[...], lim_ref[...]
    dt = jnp.exp(ldt_ref[...])
    mag = jnp.exp(lre * dt)
    ab_re = mag * jnp.cos(lim * dt)
    ab_im = mag * jnp.sin(lim * dt)
    den = lre * lre + lim * lim
    nr = ab_re - 1.0
    k_re = (nr * lre + ab_im * lim) / den
    k_im = (ab_im * lre - nr * lim) / den
    bre, bim = bre_ref[...], bim_ref[...]
    are_ref[...] = ab_re
    aim_ref[...] = ab_im
    bbre_ref[...] = k_re * bre - k_im * bim
    bbim_ref[...] = k_re * bim + k_im * bre


def s5_prep(lam_re, lam_im, log_dt, b_re, b_im):
    rep = lambda a: jnp.repeat(a, S5_GROUP, axis=0)
    rows = G_A * S5_GROUP
    ldt = jnp.broadcast_to(rep(log_dt[:, None]), (rows, S5_STATE))
    tr = lambda b: jnp.swapaxes(b, 1, 2).reshape(rows, S5_STATE)
    outs = pl.pallas_call(
        _s5_prep_kernel,
        out_shape=[jax.ShapeDtypeStruct((rows, S5_STATE), F32)] * 4,
        name="s5_prep",
    )(rep(lam_re), rep(lam_im), ldt, tr(b_re), tr(b_im))
    are, aim, bbre, bbim = (o.reshape(G_A, S5_GROUP, S5_STATE) for o in outs)
    return are[:, 0], aim[:, 0], bbre, bbim


def _block_diag(blocks, per):
    n, r, c = blocks.shape
    b = blocks.reshape(n // per, per, r, c)
    eye = jnp.eye(per, dtype=blocks.dtype)
    return jnp.einsum("hgrc,gk->hgrkc", b, eye).reshape(n // per, per * r, per * c)


def _s5_kernel(ua_ref, za_ref, h0re_ref, h0im_ref, are_ref, aim_ref, wbre_ref, wbim_ref,
               wcre_ref, wcim_ref, d_ref, wglu_ref, bglu_ref,
               ga_ref, sre_ref, sim_ref, bre_s, bim_s, *, ts):
    s = pl.program_id(1)

    @pl.when(s == 0)
    def _():
        sre_ref[...] = h0re_ref[...]
        sim_ref[...] = h0im_ref[...]

    ua = ua_ref[...]
    ua_bf = ua.astype(BF16)
    nhalf = W_A // MXU_DIM
    hw = W_S // nhalf
    for hf in range(nhalf):
        lhs = ua_bf[:, hf * MXU_DIM:(hf + 1) * MXU_DIM]
        bre_s[:, hf * hw:(hf + 1) * hw] = _dot(lhs, wbre_ref[hf])
        bim_s[:, hf * hw:(hf + 1) * hw] = _dot(lhs, wbim_ref[hf])

    for cb in range(W_S // S5_COLS):
        cols = slice(cb * S5_COLS, (cb + 1) * S5_COLS)
        ar = jnp.broadcast_to(are_ref[:, cols], (SUBLANES, S5_COLS))
        ai = jnp.broadcast_to(aim_ref[:, cols], (SUBLANES, S5_COLS))

        def step(t, carry, cols=cols, ar=ar, ai=ai):
            hr, hi = carry
            r0 = pl.multiple_of(t * SUBLANES, SUBLANES)
            nr = ar * hr - ai * hi + bre_s[pl.ds(r0, SUBLANES), cols]
            ni = ar * hi + ai * hr + bim_s[pl.ds(r0, SUBLANES), cols]
            bre_s[pl.ds(r0, SUBLANES), cols] = nr
            bim_s[pl.ds(r0, SUBLANES), cols] = ni
            return nr, ni

        hr, hi = lax.fori_loop(0, ts, step, (sre_ref[:, cols], sim_ref[:, cols]), unroll=4)
        sre_ref[:, cols] = hr
        sim_ref[:, cols] = hi

    ys = []
    for hf in range(nhalf):
        hre = bre_s[:, hf * hw:(hf + 1) * hw].astype(BF16)
        him = bim_s[:, hf * hw:(hf + 1) * hw].astype(BF16)
        ys.append(_dot(hre, wcre_ref[hf]) + _dot(him, wcim_ref[hf]))
    y = jnp.concatenate(ys, axis=-1) + d_ref[...] * ua
    y = jax.nn.gelu(y, approximate=True)
    y = y * jax.nn.sigmoid(_dot(y.astype(BF16), wglu_ref[...]) + bglu_ref[...])
    ga_ref[...] = y * jax.nn.silu(za_ref[...])


def s5_mixer(u3, h0re, h0im, w, ts):
    nbg, rows, _ = u3.shape
    rs = ts * SUBLANES
    nsteps = rows // rs
    st_spec = pl.BlockSpec((None, SUBLANES, W_S), lambda g, s: (g, 0, 0))
    nhalf = W_A // MXU_DIM
    hw = W_S // nhalf
    return pl.pallas_call(
        functools.partial(_s5_kernel, ts=ts),
        grid=(nbg, nsteps),
        in_specs=[pl.BlockSpec((None, rs, W_A), lambda g, s: (g, s, 0)),
                  pl.BlockSpec((None, rs, W_A), lambda g, s: (g, s, 1)),
                  st_spec, st_spec,
                  _const_spec((1, W_S)), _const_spec((1, W_S)),
                  _const_spec((nhalf, MXU_DIM, hw)), _const_spec((nhalf, MXU_DIM, hw)),
                  _const_spec((nhalf, hw, MXU_DIM)), _const_spec((nhalf, hw, MXU_DIM)),
                  _const_spec((1, W_A)), _const_spec((W_A, W_A)), _const_spec((1, W_A))],
        out_specs=[pl.BlockSpec((None, rs, W_A), lambda g, s: (g, s, 0)), st_spec, st_spec],
        out_shape=[jax.ShapeDtypeStruct((nbg, rows, W_A), F32),
                   jax.ShapeDtypeStruct((nbg, SUBLANES, W_S), F32),
                   jax.ShapeDtypeStruct((nbg, SUBLANES, W_S), F32)],
        scratch_shapes=[pltpu.VMEM((rs, W_S), F32), pltpu.VMEM((rs, W_S), F32)],
        compiler_params=_cparams(("parallel", "arbitrary")),
        name="s5_mixer",
    )(u3, u3, h0re, h0im, w["s5_are"], w["s5_aim"], w["s5_wbre"], w["s5_wbim"],
      w["s5_wcre"], w["s5_wcim"], w["s5_d"], w["s5_wglu"], w["s5_bglu"])


def _conv_step(c, x_ref, buf0_ref, xpad, cw_ref, cb_ref, buf1_ref, rc):
    @pl.when(c == 0)
    def _():
        xpad[0:HIST, :] = buf0_ref[...]

    @pl.when(c > 0)
    def _():
        xpad[0:HIST, :] = xpad[rc:rc + HIST, :]

    xpad[HIST:HIST + rc, :] = x_ref[...]
    buf1_ref[...] = xpad[rc:rc + HIST, :]
    out = cb_ref[...]
    for tap in range(CONV_K):
        out = out + xpad[tap * SUBLANES:tap * SUBLANES + rc, :] * cw_ref[tap:tap + 1, :]
    return out


def _mlstm_kernel(xb_ref, zb_ref, buf0_ref, c0_hbm, n0_ref, m0_ref, cw_ref, cb_ref,
                  wq_ref, wk_ref, wv_ref, wif_ref, bif_ref, nw_ref, skip_ref,
                  gb_ref, c1_hbm, n1_ref, m1_ref, buf1_ref,
                  xpad, xc_s, q_s, k_s, v_s, g_s, h_s, c_s, sem, *, tc):
    g = pl.program_id(0)
    c = pl.program_id(1)
    rc = tc * SUBLANES
    nstate = SUBLANES * H_B

    @pl.when(c == 0)
    def _():
        cp = pltpu.make_async_copy(c0_hbm.at[pl.ds(g * nstate, nstate)], c_s, sem)
        cp.start()
        n1_ref[...] = n0_ref[...]
        m1_ref[...] = m0_ref[...]
        cp.wait()

    xc = jax.nn.silu(_conv_step(c, xb_ref, buf0_ref, xpad, cw_ref, cb_ref, buf1_ref, rc))
    xc_s[...] = xc
    xc_bf = xc.astype(BF16)
    xb_bf = xb_ref[...].astype(BF16)
    gates = bif_ref[...]
    for h in range(H_B):
        cols = slice(h * DH_B, (h + 1) * DH_B)
        for i, (src, w_ref, dst) in enumerate(((xc_bf, wq_ref, q_s), (xc_bf, wk_ref, k_s), (xb_bf, wv_ref, v_s))):
            r = _dot(src[:, cols], w_ref[h])
            _put_cols(dst, h * DH_B, r)
            gates = gates + _dot(r.astype(BF16), wif_ref[i * W_B + h * DH_B:i * W_B + (h + 1) * DH_B, :])
    lane = lax.broadcasted_iota(jnp.int32, (rc, LANES), 1)
    gates = jnp.where(lane < H_B, gates, -_softplus(-gates))
    is_f = lax.broadcasted_iota(jnp.int32, (SUBLANES, LANES), 1) >= H_B
    run = jnp.zeros((SUBLANES, LANES), F32)
    for t in range(tc):
        cur = gates[t * SUBLANES:(t + 1) * SUBLANES, :]
        run = jnp.where(is_f, run + cur, cur)
        g_s[t * SUBLANES:(t + 1) * SUBLANES, :] = run

    ri = lax.broadcasted_iota(jnp.int32, (tc, tc), 0)
    ci = lax.broadcasted_iota(jnp.int32, (tc, tc), 1)
    causal = ri >= ci
    eye = ri == ci
    m_row = lax.broadcasted_iota(jnp.int32, (SUBLANES, LANES), 0)
    m_lane = lax.broadcasted_iota(jnp.int32, (SUBLANES, LANES), 1)
    kscale = DH_B ** -0.5

    def per_batch(b, carry):
        gt = g_s[pl.ds(b, tc, stride=SUBLANES), :]
        for h in range(H_B):
            cols = slice(h * DH_B, (h + 1) * DH_B)
            ig = gt[:, h:h + 1]
            bc = gt[:, H_B + h:H_B + h + 1]
            a_row = jnp.sum(jnp.where(eye, ig - bc, 0.0), axis=0, keepdims=True)
            dlog = jnp.where(causal, bc + a_row, -jnp.inf)
            m0 = m1_ref[pl.ds(b, 1), h:h + 1]
            inter = bc + m0
            m = jnp.maximum(inter, jnp.max(dlog, axis=1, keepdims=True))
            p = jnp.exp(dlog - m)
            q = _get_batch(q_s, b, tc, h * DH_B, DH_B)
            k = _get_batch(k_s, b, tc, h * DH_B, DH_B) * kscale
            v_bf = _get_batch(v_s, b, tc, h * DH_B, DH_B).astype(BF16)
            q_bf = q.astype(BF16)
            sc = _dot_nt(q_bf, k.astype(BF16)) * p
            w_inter = jnp.exp(inter - m)
            c0 = c_s[b * H_B + h]
            n0 = n1_ref[pl.ds(b, 1), cols]
            num = _dot(sc.astype(BF16), v_bf) + w_inter * _dot(q_bf, c0.astype(BF16))
            den = jnp.sum(sc, axis=1, keepdims=True) + w_inter * jnp.sum(q * n0, axis=1, keepdims=True)
            hv = num / jnp.maximum(jnp.abs(den), jnp.exp(-m))
            _put_batch(h_s, b, tc, h * DH_B, _head_norm(hv))
            m_end = m[tc - 1:tc, :]
            w_end = jnp.exp(bc[tc - 1:tc, :] - bc + ig - m_end)
            decay = jnp.exp(inter[tc - 1:tc, :] - m_end)
            kw = k * w_end
            c_s[b * H_B + h] = decay * c0 + _dot_tn(kw.astype(BF16), v_bf)
            n1_ref[pl.ds(b, 1), cols] = decay * n0 + jnp.sum(kw, axis=0, keepdims=True)
            m1_ref[...] = jnp.where((m_row == b) & (m_lane == h), m_end, m1_ref[...])
        return carry

    lax.fori_loop(0, SUBLANES, per_batch, 0)
    y = _get_all(h_s) * nw_ref[...] + skip_ref[...] * xc_s[...]
    gb_ref[...] = y * jax.nn.silu(zb_ref[...])

    @pl.when(c == pl.num_programs(1) - 1)
    def _():
        cp = pltpu.make_async_copy(c_s, c1_hbm.at[pl.ds(g * nstate, nstate)], sem)
        cp.start()
        cp.wait()


def mlstm_mixer(u3, buf0, c0, n0, m0, w, tc):
    nbg, rows, _ = u3.shape
    rc = tc * SUBLANES
    nch = rows // rc
    nstate = SUBLANES * H_B
    grp = lambda shape: pl.BlockSpec((None,) + shape, lambda g, c: (g, 0, 0))
    blocked = pltpu.VMEM((W_B // LANES, rc, LANES), F32)
    return pl.pallas_call(
        functools.partial(_mlstm_kernel, tc=tc),
        grid=(nbg, nch),
        in_specs=[pl.BlockSpec((None, rc, W_B), lambda g, c: (g, c, 1)),
                  pl.BlockSpec((None, rc, W_B), lambda g, c: (g, c, 2)),
                  grp((HIST, W_B)),
                  pl.BlockSpec(memory_space=pl.ANY),
                  grp((SUBLANES, W_B)), grp((SUBLANES, LANES)),
                  _const_spec((CONV_K, W_B)), _const_spec((1, W_B)),
                  _const_spec((H_B, DH_B, DH_B)), _const_spec((H_B, DH_B, DH_B)), _const_spec((H_B, DH_B, DH_B)),
                  _const_spec((3 * W_B, LANES)), _const_spec((1, LANES)),
                  _const_spec((1, W_B)), _const_spec((1, W_B))],
        out_specs=[pl.BlockSpec((None, rc, W_B), lambda g, c: (g, c, 0)),
                   pl.BlockSpec(memory_space=pl.ANY),
                   grp((SUBLANES, W_B)), grp((SUBLANES, LANES)), grp((HIST, W_B))],
        out_shape=[jax.ShapeDtypeStruct((nbg, rows, W_B), F32),
                   jax.ShapeDtypeStruct(c0.shape, F32),
                   jax.ShapeDtypeStruct((nbg, SUBLANES, W_B), F32),
                   jax.ShapeDtypeStruct((nbg, SUBLANES, LANES), F32),
                   jax.ShapeDtypeStruct((nbg, HIST, W_B), F32)],
        scratch_shapes=[pltpu.VMEM((rc + HIST, W_B), F32),
                        pltpu.VMEM((rc, W_B), F32), blocked, blocked, blocked,
                        pltpu.VMEM((rc, LANES), F32), blocked,
                        pltpu.VMEM((nstate, DH_B, DH_B), F32),
                        pltpu.SemaphoreType.DMA(())],
        compiler_params=_cparams(("parallel", "arbitrary")),
        name="mlstm_mixer",
    )(u3, u3, buf0, c0, n0, m0, w["ml_cw"], w["ml_cb"], w["ml_wq"], w["ml_wk"], w["ml_wv"],
      w["ml_wif"], w["ml_bif"], w["ml_nw"], w["ml_skip"])


def _ret_log_decay(h):
    return float(np.log1p(-np.exp2(-5.0 - h)))


def _retention_kernel(q_ref, k_ref, v_ref, zc_ref, cos_ref, sin_ref, s0_hbm, nw_ref,
                      gc_ref, s1_hbm, q_s, k_s, v_s, o_s, st_s, sem, *, tc):
    g = pl.program_id(0)
    c = pl.program_id(1)
    nstate = SUBLANES * H_C

    @pl.when(c == 0)
    def _():
        cp = pltpu.make_async_copy(s0_hbm.at[pl.ds(g * nstate, nstate)], st_s, sem)
        cp.start()
        cp.wait()

    cos, sin = cos_ref[...], sin_ref[...]
    _put_cols(v_s, 0, v_ref[...])
    for h in range(H_C):
        cols = slice(h * DK_C, (h + 1) * DK_C)
        for src, dst, scale in ((q_ref, q_s, None), (k_ref, k_s, DK_C ** -0.5)):
            x = src[:, cols]
            r = x * cos + pltpu.roll(x, DK_C // 2, 1) * sin
            dst[h] = r if scale is None else r * scale

    ri = lax.broadcasted_iota(jnp.int32, (tc, tc), 0)
    ci = lax.broadcasted_iota(jnp.int32, (tc, tc), 1)
    diff = (ri - ci).astype(F32)
    tcol = lax.broadcasted_iota(jnp.int32, (tc, 1), 0).astype(F32)
    dmask, xi, zeta, gfull = [], [], [], []
    for h in range(H_C):
        lg = _ret_log_decay(h)
        dmask.append(jnp.where(diff >= 0, jnp.exp(lg * jnp.maximum(diff, 0.0)), 0.0))
        xi.append(jnp.exp(lg * (tcol + 1.0)))
        zeta.append(jnp.exp(lg * (tc - 1.0 - tcol)))
        gfull.append(float(np.exp(np.float32(lg) * np.float32(tc))))

    def per_batch(b, carry):
        for h in range(H_C):
            kc = slice(h * DK_C, (h + 1) * DK_C)
            vc = slice(h * DV_C, (h + 1) * DV_C)
            q_bf = _get_batch(q_s, b, tc, h * DK_C, DK_C).astype(BF16)
            k = _get_batch(k_s, b, tc, h * DK_C, DK_C)
            v_bf = _get_batch(v_s, b, tc, h * DV_C, DV_C).astype(BF16)
            s0 = st_s[b * H_C + h]
            inner = _dot_nt(q_bf, k.astype(BF16)) * dmask[h]
            o = _dot(inner.astype(BF16), v_bf) + xi[h] * _dot(q_bf, s0.astype(BF16))
            _put_batch(o_s, b, tc, h * DV_C, _head_norm(o))
            st_s[b * H_C + h] = gfull[h] * s0 + _dot_tn((k * zeta[h]).astype(BF16), v_bf)
        return carry

    lax.fori_loop(0, SUBLANES, per_batch, 0)
    gc_ref[...] = _get_all(o_s) * nw_ref[...] * jax.nn.silu(zc_ref[...])

    @pl.when(c == pl.num_programs(1) - 1)
    def _():
        cp = pltpu.make_async_copy(st_s, s1_hbm.at[pl.ds(g * nstate, nstate)], sem)
        cp.start()
        cp.wait()


def retention_mixer(u3, cos, sin, s0, w, tc):
    nbg, rows, _ = u3.shape
    rc = tc * SUBLANES
    nch = rows // rc
    nstate = SUBLANES * H_C
    return pl.pallas_call(
        functools.partial(_retention_kernel, tc=tc),
        grid=(nbg, nch),
        in_specs=[pl.BlockSpec((None, rc, QK_C), lambda g, c: (g, c, 0)),
                  pl.BlockSpec((None, rc, QK_C), lambda g, c: (g, c, 1)),
                  pl.BlockSpec((None, rc, W_C), lambda g, c: (g, c, 1)),
                  pl.BlockSpec((None, rc, W_C), lambda g, c: (g, c, 2)),
                  pl.BlockSpec((None, rc, DK_C), lambda g, c: (g, c, 0)),
                  pl.BlockSpec((None, rc, DK_C), lambda g, c: (g, c, 0)),
                  pl.BlockSpec(memory_space=pl.ANY),
                  _const_spec((1, W_C))],
        out_specs=[pl.BlockSpec((None, rc, W_C), lambda g, c: (g, c, 0)),
                   pl.BlockSpec(memory_space=pl.ANY)],
        out_shape=[jax.ShapeDtypeStruct((nbg, rows, W_C), F32),
                   jax.ShapeDtypeStruct(s0.shape, F32)],
        scratch_shapes=[pltpu.VMEM((QK_C // LANES, rc, LANES), F32), pltpu.VMEM((QK_C // LANES, rc, LANES), F32),
                        pltpu.VMEM((W_C // LANES, rc, LANES), F32), pltpu.VMEM((W_C // LANES, rc, LANES), F32),
                        pltpu.VMEM((nstate, DK_C, DV_C), F32),
                        pltpu.SemaphoreType.DMA(())],
        compiler_params=_cparams(("parallel", "arbitrary")),
        name="retention_mixer",
    )(u3, u3, u3, u3, cos, sin, s0, w["ret_nw"])


def _rglru_kernel(xd_ref, zd_ref, buf0_ref, h0_ref, cw_ref, cb_ref, wa_ref, ba_ref, wx_ref, bx_ref, lam_ref,
                  gd_ref, h1_ref, buf1_ref, xpad, a_s, b_s, *, tc):
    c = pl.program_id(1)
    rc = tc * SUBLANES

    @pl.when(c == 0)
    def _():
        h1_ref[...] = h0_ref[...]

    xc = _conv_step(c, xd_ref, buf0_ref, xpad, cw_ref, cb_ref, buf1_ref, rc)
    xc_bf = xc.astype(BF16)
    nblk = W_D // MXU_DIM
    rg, ig = [], []
    for j in range(nblk):
        lhs = xc_bf[:, j * MXU_DIM:(j + 1) * MXU_DIM]
        rg.append(_dot(lhs, wa_ref[j]))
        ig.append(_dot(lhs, wx_ref[j]))
    r = jax.nn.sigmoid(jnp.concatenate(rg, axis=-1) + ba_ref[...])
    i = jax.nn.sigmoid(jnp.concatenate(ig, axis=-1) + bx_ref[...])
    log_a = -LRU_C * r * _softplus(-lam_ref[...])
    a = jnp.exp(log_a)
    a_s[...] = a
    b_s[...] = jnp.sqrt(-jnp.tanh(log_a) * (a * a + 1.0)) * (i * xc)

    def step(t, h):
        r0 = pl.multiple_of(t * SUBLANES, SUBLANES)
        h = a_s[pl.ds(r0, SUBLANES), :] * h + b_s[pl.ds(r0, SUBLANES), :]
        b_s[pl.ds(r0, SUBLANES), :] = h
        return h

    h1_ref[...] = lax.fori_loop(0, tc, step, h1_ref[...], unroll=4)
    gd_ref[...] = b_s[...] * jax.nn.silu(zd_ref[...])


def rglru_mixer(u3, buf0, h0, w, tc):
    nbg, rows, _ = u3.shape
    rc = tc * SUBLANES
    nch = rows // rc
    nblk = W_D // MXU_DIM
    grp = lambda shape: pl.BlockSpec((None,) + shape, lambda g, c: (g, 0, 0))
    return pl.pallas_call(
        functools.partial(_rglru_kernel, tc=tc),
        grid=(nbg, nch),
        in_specs=[pl.BlockSpec((None, rc, W_D), lambda g, c: (g, c, 3)),
                  pl.BlockSpec((None, rc, W_D), lambda g, c: (g, c, 4)),
                  grp((HIST, W_D)), grp((SUBLANES, W_D)),
                  _const_spec((CONV_K, W_D)), _const_spec((1, W_D)),
                  _const_spec((nblk, MXU_DIM, MXU_DIM)), _const_spec((1, W_D)),
                  _const_spec((nblk, MXU_DIM, MXU_DIM)), _const_spec((1, W_D)),
                  _const_spec((1, W_D))],
        out_specs=[pl.BlockSpec((None, rc, W_D), lambda g, c: (g, c, 0)),
                   grp((SUBLANES, W_D)), grp((HIST, W_D))],
        out_shape=[jax.ShapeDtypeStruct((nbg, rows, W_D), F32),
                   jax.ShapeDtypeStruct((nbg, SUBLANES, W_D), F32),
                   jax.ShapeDtypeStruct((nbg, HIST, W_D), F32)],
        scratch_shapes=[pltpu.VMEM((rc + HIST, W_D), F32), pltpu.VMEM((rc, W_D), F32), pltpu.VMEM((rc, W_D), F32)],
        compiler_params=_cparams(("parallel", "arbitrary")),
        name="rglru_mixer",
    )(u3, u3, buf0, h0, w["lru_cw"], w["lru_cb"], w["lru_wa"], w["lru_ba"], w["lru_wx"], w["lru_bx"], w["lru_lam"])


def _to_tm(x):
    b, t, c = x.shape
    return x.reshape(b // SUBLANES, SUBLANES, t, c).swapaxes(1, 2).reshape(b // SUBLANES, t * SUBLANES, c)


def _from_tm(x, t):
    nbg, _, c = x.shape
    return x.reshape(nbg, t, SUBLANES, c).swapaxes(1, 2).reshape(nbg * SUBLANES, t, c)


def _rope_tables(pos):
    half = DK_C // 2
    inv = ROPE_BASE ** (-jnp.arange(half, dtype=F32) / half)
    ang = pos[..., None] * inv
    cos = jnp.cos(ang)
    sin = jnp.sin(ang)
    cos = jnp.concatenate([cos, cos], axis=-1)
    sin = jnp.concatenate([-sin, sin], axis=-1)
    rep = lambda a: jnp.repeat(a, SUBLANES, axis=1)
    return rep(cos), rep(sin)


def _prep_weights(p):
    w = {}
    w["norm0"] = p["norm_w"][0][None]
    w["norm1"] = p["norm_w"][1][None]
    w["final_norm"] = p["final_norm_w"][None]
    w["ev_w_in"] = p["ev_w_in"][0].astype(BF16)
    w["ev_w_out_a"] = p["ev_w_out"][0][:W_A].astype(BF16)
    w["ev_w_out_b"] = p["ev_w_out"][0][W_A:].astype(BF16)
    are, aim, bbre, bbim = s5_prep(p["s5_lambda_re"][0], p["s5_lambda_im"][0], p["s5_log_dt"][0],
                                   p["s5_b_re"][0], p["s5_b_im"][0])
    per = MXU_DIM // S5_GROUP
    w["s5_are"] = are.reshape(1, W_S)
    w["s5_aim"] = aim.reshape(1, W_S)
    w["s5_wbre"] = _block_diag(bbre, per).astype(BF16)
    w["s5_wbim"] = _block_diag(bbim, per).astype(BF16)
    w["s5_wcre"] = _block_diag(jnp.swapaxes(p["s5_c_re"][0], 1, 2), per).astype(BF16)
    w["s5_wcim"] = _block_diag(-jnp.swapaxes(p["s5_c_im"][0], 1, 2), per).astype(BF16)
    w["s5_d"] = p["s5_d"][0][None]
    w["s5_wglu"] = p["s5_w_glu"][0].astype(BF16)
    w["s5_bglu"] = p["s5_b_glu"][0][None]
    w["ml_cw"] = p["ml_conv_w"][0]
    w["ml_cb"] = p["ml_conv_b"][0][None]
    w["ml_wq"] = p["ml_wq"][0].astype(BF16)
    w["ml_wk"] = p["ml_wk"][0].astype(BF16)
    w["ml_wv"] = p["ml_wv"][0].astype(BF16)
    w["ml_wif"] = jnp.pad(p["ml_w_if"][0], ((0, 0), (0, LANES - 2 * H_B))).astype(BF16)
    w["ml_bif"] = jnp.pad(p["ml_b_if"][0], (0, LANES - 2 * H_B))[None]
    w["ml_nw"] = p["ml_norm_w"][0][None]
    w["ml_skip"] = p["ml_skip"][0][None]
    w["od_w_in"] = p["od_w_in"][0].astype(BF16)
    w["od_w_out_c"] = p["od_w_out"][0][:W_C].astype(BF16)
    w["od_w_out_d"] = p["od_w_out"][0][W_C:].astype(BF16)
    w["ret_nw"] = p["ret_norm_w"][0][None]
    w["lru_cw"] = p["lru_conv_w"][0]
    w["lru_cb"] = p["lru_conv_b"][0][None]
    w["lru_wa"] = _block_diag(p["lru_w_a"][0], MXU_DIM // BD_D).astype(BF16)
    w["lru_ba"] = p["lru_b_a"][0][None]
    w["lru_wx"] = _block_diag(p["lru_w_x"][0], MXU_DIM // BD_D).astype(BF16)
    w["lru_bx"] = p["lru_b_x"][0][None]
    w["lru_lam"] = p["lru_lambda"][0][None]
    return w


def _conv_to_tm(buf):
    return _to_tm(buf)


def _trunk(x3, pos, st, w, tc):
    nbg, rows, d = x3.shape
    flat = lambda a: a.reshape(nbg * rows, a.shape[-1])
    unflat = lambda a: a.reshape(nbg, rows, a.shape[-1])
    new = {}
    u = unflat(proj_in(flat(x3), w["norm0"], w["ev_w_in"]))
    ga, new["s5_re"], new["s5_im"] = s5_mixer(u, st["s5_re"], st["s5_im"], w, tc)
    gb, new["ml_c"], new["ml_n"], new["ml_m"], new["ml_conv"] = mlstm_mixer(
        u, st["ml_conv"], st["ml_c"], st["ml_n"], st["ml_m"], w, tc)
    h1 = proj_out(flat(x3), flat(ga), flat(gb), w["ev_w_out_a"], w["ev_w_out_b"], w["final_norm"], False)
    u = unflat(proj_in(h1, w["norm1"], w["od_w_in"]))
    cos, sin = _rope_tables(pos)
    gc, new["ret"] = retention_mixer(u, cos, sin, st["ret"], w, tc)
    gd, new["lru_h"], new["lru_conv"] = rglru_mixer(u, st["lru_conv"], st["lru_h"], w, tc)
    y = proj_out(h1, flat(gc), flat(gd), w["od_w_out_c"], w["od_w_out_d"], w["final_norm"], True)
    return unflat(y), new


def _states_in(s5_re, s5_im, ml_c, ml_n, ml_m, ml_conv, ret, lru_h, lru_conv):
    b = s5_re.shape[0]
    nbg = b // SUBLANES
    return dict(
        s5_re=s5_re.reshape(nbg, SUBLANES, W_S),
        s5_im=s5_im.reshape(nbg, SUBLANES, W_S),
        ml_c=ml_c.reshape(b * H_B, DH_B, DH_B),
        ml_n=ml_n.reshape(nbg, SUBLANES, W_B),
        ml_m=jnp.pad(ml_m, ((0, 0), (0, LANES - H_B))).reshape(nbg, SUBLANES, LANES),
        ml_conv=_conv_to_tm(ml_conv),
        ret=ret.reshape(b * H_C, DK_C, DV_C),
        lru_h=lru_h.reshape(nbg, SUBLANES, W_D),
        lru_conv=_conv_to_tm(lru_conv),
    )


def _states_out(st, sel=None):
    nbg = st["s5_re"].shape[0]
    b = nbg * SUBLANES
    out = dict(
        s5_re=st["s5_re"].reshape(b, G_A, S5_STATE),
        s5_im=st["s5_im"].reshape(b, G_A, S5_STATE),
        ml_c=st["ml_c"].reshape(b, H_B, DH_B, DH_B),
        ml_n=st["ml_n"].reshape(b, H_B, DH_B),
        ml_m=st["ml_m"].reshape(b, LANES)[:, :H_B],
        ml_conv=_from_tm(st["ml_conv"], CONV_K - 1),
        ret=st["ret"].reshape(b, H_C, DK_C, DV_C),
        lru_h=st["lru_h"].reshape(b, W_D),
        lru_conv=_from_tm(st["lru_conv"], CONV_K - 1),
    )
    if sel is not None:
        out = {k: v[sel] for k, v in out.items()}
    return out


_STATE_ORDER = ("s5_re", "s5_im", "ml_c", "ml_n", "ml_m", "ml_conv", "ret", "lru_h", "lru_conv")


def kernel(x_prompt, x_sample, state_s5_re, state_s5_im, state_ml_c, state_ml_n, state_ml_m, state_ml_conv, state_ret, state_lru_h, state_lru_conv, meta, norm_w, final_norm_w, ev_w_in, ev_w_out, s5_lambda_re, s5_lambda_im, s5_log_dt, s5_b_re, s5_b_im, s5_c_re, s5_c_im, s5_d, s5_w_glu, s5_b_glu, ml_conv_w, ml_conv_b, ml_wq, ml_wk, ml_wv, ml_w_if, ml_b_if, ml_norm_w, ml_skip, od_w_in, od_w_out, ret_norm_w, lru_conv_w, lru_conv_b, lru_w_a, lru_b_a, lru_w_x, lru_b_x, lru_lambda):
    p = dict(norm_w=norm_w, final_norm_w=final_norm_w, ev_w_in=ev_w_in, ev_w_out=ev_w_out,
             s5_lambda_re=s5_lambda_re, s5_lambda_im=s5_lambda_im, s5_log_dt=s5_log_dt,
             s5_b_re=s5_b_re, s5_b_im=s5_b_im, s5_c_re=s5_c_re, s5_c_im=s5_c_im, s5_d=s5_d,
             s5_w_glu=s5_w_glu, s5_b_glu=s5_b_glu, ml_conv_w=ml_conv_w, ml_conv_b=ml_conv_b,
             ml_wq=ml_wq, ml_wk=ml_wk, ml_wv=ml_wv, ml_w_if=ml_w_if, ml_b_if=ml_b_if,
             ml_norm_w=ml_norm_w, ml_skip=ml_skip, od_w_in=od_w_in, od_w_out=od_w_out,
             ret_norm_w=ret_norm_w, lru_conv_w=lru_conv_w, lru_conv_b=lru_conv_b,
             lru_w_a=lru_w_a, lru_b_a=lru_b_a, lru_w_x=lru_w_x, lru_b_x=lru_b_x, lru_lambda=lru_lambda)
    assert ev_w_in.shape[0] == 1 and od_w_in.shape[0] == 1, "two-layer trunk"
    w = _prep_weights(p)
    bp, tp, _ = x_prompt.shape
    bs, ts, _ = x_sample.shape
    assert ts == N_META and bp % SUBLANES == 0 and bs % SUBLANES == 0 and tp % CHUNK == 0

    meta_b = jnp.broadcast_to(meta[None], (bp, N_META, D_MODEL))
    x_short = _to_tm(jnp.concatenate([x_sample, meta_b], axis=0))
    given = (state_s5_re[0], state_s5_im[0], state_ml_c[0], state_ml_n[0], state_ml_m[0], state_ml_conv[0],
             state_ret[0], state_lru_h[0], state_lru_conv[0])
    both = [jnp.concatenate([s, jnp.zeros((bp,) + s.shape[1:], s.dtype)], axis=0) for s in given]
    tpos = jnp.arange(N_META, dtype=F32)
    pos_short = jnp.concatenate([jnp.broadcast_to((N_META + PAST_LEN) + tpos, (bs // SUBLANES, N_META)),
                                 jnp.broadcast_to(tpos, (bp // SUBLANES, N_META))], axis=0)
    y_short, st_short = _trunk(x_short, pos_short, _states_in(*both), w, N_META)
    y_sample = _from_tm(y_short, N_META)[:bs]
    new_s = _states_out(st_short, slice(0, bs))

    nbs = bs // SUBLANES
    st_meta = {}
    for name, val in st_short.items():
        per_group = val.shape[0] // (x_short.shape[0])
        st_meta[name] = val[nbs * per_group:]
    pos_long = jnp.broadcast_to(N_META + jnp.arange(tp, dtype=F32), (bp // SUBLANES, tp))
    y_long, st_long = _trunk(_to_tm(x_prompt), pos_long, st_meta, w, CHUNK)
    y_prompt = _from_tm(y_long, tp)
    new_p = _states_out(st_long)

    return ((y_prompt, y_sample)
            + tuple(new_p[k][None] for k in _STATE_ORDER)
            + tuple(new_s[k][None] for k in _STATE_ORDER))
```

```python
import functools
import math

import numpy as np
import jax
import jax.numpy as jnp
from jax import lax
from jax.experimental import pallas as pl
from jax.experimental.pallas import tpu as pltpu

F32 = jnp.float32
BF16 = jnp.bfloat16

SUBLANES = 8
LANES = 128
MXU_DIM = 256
VMEM_LIMIT = 56 * 1024 * 1024

D_MODEL = 1024
CHUNK = 64
N_META = 16
PAST_LEN = 2048
EPS = 1e-6
CONV_K = 4
S5_GROUP = 16
S5_STATE = 64
W_A = D_MODEL // 2
G_A = W_A // S5_GROUP
W_S = G_A * S5_STATE
W_B = D_MODEL
H_B = 4
DH_B = W_B // H_B
W_C = D_MODEL
H_C = 4
DV_C = W_C // H_C
DK_C = DV_C // 2
QK_C = H_C * DK_C
ROPE_BASE = 10000.0
W_D = D_MODEL
H_D = 8
BD_D = W_D // H_D
LRU_C = 8.0
EV_IN = 2 * W_A + 2 * W_B
OD_IN = 2 * QK_C + 2 * W_C + 2 * W_D
HIST = (CONV_K - 1) * SUBLANES

ROW_TILE = 512
S5_COLS = 512


def _cparams(sem):
    return pltpu.CompilerParams(dimension_semantics=sem, vmem_limit_bytes=VMEM_LIMIT)


def _const_spec(shape):
    nd = len(shape)
    return pl.BlockSpec(shape, lambda *_: (0,) * nd, pipeline_mode=pl.Buffered(1))


def _dot(a, b):
    return jnp.dot(a, b, preferred_element_type=F32)


def _dot_nt(a, b):
    return lax.dot_general(a, b, (((1,), (1,)), ((), ())), preferred_element_type=F32)


def _dot_tn(a, b):
    return lax.dot_general(a, b, (((0,), (0,)), ((), ())), preferred_element_type=F32)


def _rms(x, g):
    return x * lax.rsqrt(jnp.mean(x * x, axis=-1, keepdims=True) + EPS) * g


def _softplus(x):
    return jnp.maximum(x, 0.0) + jnp.log1p(jnp.exp(-jnp.abs(x)))


def _put_cols(dst3, col0, val):
    for j in range(val.shape[1] // LANES):
        dst3[col0 // LANES + j] = val[:, j * LANES:(j + 1) * LANES]


def _get_all(src3):
    return jnp.concatenate([src3[j] for j in range(src3.shape[0])], axis=-1)


def _get_batch(src3, b, tc, col0, width):
    parts = [src3[col0 // LANES + j, pl.ds(b, tc, stride=SUBLANES), :] for j in range(width // LANES)]
    return parts[0] if len(parts) == 1 else jnp.concatenate(parts, axis=-1)


def _put_batch(dst3, b, tc, col0, val):
    for j in range(val.shape[1] // LANES):
        dst3[col0 // LANES + j, pl.ds(b, tc, stride=SUBLANES), :] = val[:, j * LANES:(j + 1) * LANES]


def _get_batches(src3, tc, col0, width):
    return jnp.stack([_get_batch(src3, b, tc, col0, width) for b in range(SUBLANES)], axis=0)


def _put_batches(dst3, tc, col0, val):
    for b in range(SUBLANES):
        _put_batch(dst3, b, tc, col0, val[b])


def _bdot(a, b):
    return lax.dot_general(a, b, (((2,), (1,)), ((0,), (0,))), preferred_element_type=F32)


def _bdot_nt(a, b):
    return lax.dot_general(a, b, (((2,), (2,)), ((0,), (0,))), preferred_element_type=F32)


def _bdot_tn(a, b):
    return lax.dot_general(a, b, (((1,), (1,)), ((0,), (0,))), preferred_element_type=F32)


def _head_norm(h):
    mu = jnp.mean(h, axis=-1, keepdims=True)
    d = h - mu
    var = jnp.mean(d * d, axis=-1, keepdims=True)
    return d * lax.rsqrt(var + EPS)


def _proj_in_kernel(x_ref, g_ref, w_ref, o_ref):
    y = _rms(x_ref[...], g_ref[...])
    o_ref[...] = _dot(y.astype(BF16), w_ref[...])


def proj_in(x2, g, w):
    rows, d = x2.shape
    n = w.shape[1]
    tm = min(ROW_TILE, rows)
    return pl.pallas_call(
        _proj_in_kernel,
        grid=(rows // tm,),
        in_specs=[pl.BlockSpec((tm, d), lambda i: (i, 0)),
                  _const_spec((1, d)),
                  _const_spec((d, n))],
        out_specs=pl.BlockSpec((tm, n), lambda i: (i, 0)),
        out_shape=jax.ShapeDtypeStruct((rows, n), F32),
        compiler_params=_cparams(("parallel",)),
        name="proj_in",
    )(x2, g, w)


def _proj_out_kernel(h_ref, a_ref, b_ref, wa_ref, wb_ref, g_ref, o_ref, *, final):
    y = h_ref[...] + _dot(a_ref[...].astype(BF16), wa_ref[...]) + _dot(b_ref[...].astype(BF16), wb_ref[...])
    if final:
        y = _rms(y, g_ref[...])
    o_ref[...] = y


def proj_out(h2, a2, b2, wa, wb, g, final):
    rows, d = h2.shape
    ka, kb = a2.shape[1], b2.shape[1]
    tm = min(ROW_TILE, rows)
    return pl.pallas_call(
        functools.partial(_proj_out_kernel, final=final),
        grid=(rows // tm,),
        in_specs=[pl.BlockSpec((tm, d), lambda i: (i, 0)),
                  pl.BlockSpec((tm, ka), lambda i: (i, 0)),
                  pl.BlockSpec((tm, kb), lambda i: (i, 0)),
                  _const_spec((ka, d)),
                  _const_spec((kb, d)),
                  _const_spec((1, d))],
        out_specs=pl.BlockSpec((tm, d), lambda i: (i, 0)),
        out_shape=jax.ShapeDtypeStruct((rows, d), F32),
        compiler_params=_cparams(("parallel",)),
        name="proj_out",
    )(h2, a2, b2, wa, wb, g)


def _s5_prep_kernel(lre_ref, lim_ref, ldt_ref, bre_ref, bim_ref, are_ref, aim_ref, bbre_ref, bbim_ref):
    lre, lim = lre_ref[...], lim_ref[...]
    dt = jnp.exp(ldt_ref[...])
    mag = jnp.exp(lre * dt)
    ab_re = mag * jnp.cos(lim * dt)
    ab_im = mag * jnp.sin(lim * dt)
    den = lre * lre + lim * lim
    nr = ab_re - 1.0
    k_re = (nr * lre + ab_im * lim) / den
    k_im = (ab_im * lre - nr * lim) / den
    bre, bim = bre_ref[...], bim_ref[...]
    are_ref[...] = ab_re
    aim_ref[...] = ab_im
    bbre_ref[...] = k_re * bre - k_im * bim
    bbim_ref[...] = k_re * bim + k_im * bre


def s5_prep(lam_re, lam_im, log_dt, b_re, b_im):
    rep = lambda a: jnp.repeat(a, S5_GROUP, axis=0)
    rows = G_A * S5_GROUP
    ldt = jnp.broadcast_to(rep(log_dt[:, None]), (rows, S5_STATE))
    tr = lambda b: jnp.swapaxes(b, 1, 2).reshape(rows, S5_STATE)
    outs = pl.pallas_call(
        _s5_prep_kernel,
        out_shape=[jax.ShapeDtypeStruct((rows, S5_STATE), F32)] * 4,
        name="s5_prep",
    )(rep(lam_re), rep(lam_im), ldt, tr(b_re), tr(b_im))
    are, aim, bbre, bbim = (o.reshape(G_A, S5_GROUP, S5_STATE) for o in outs)
    return are[:, 0], aim[:, 0], bbre, bbim


def _block_diag(blocks, per):
    n, r, c = blocks.shape
    b = blocks.reshape(n // per, per, r, c)
    eye = jnp.eye(per, dtype=blocks.dtype)
    return jnp.einsum("hgrc,gk->hgrkc", b, eye).reshape(n // per, per * r, per * c)


def _s5_kernel(ua_ref, za_ref, h0re_ref, h0im_ref, are_ref, aim_ref, wbre_ref, wbim_ref,
               wcre_ref, wcim_ref, d_ref, wglu_ref, bglu_ref,
               ga_ref, sre_ref, sim_ref, bre_s, bim_s, *, ts):
    s = pl.program_id(1)

    @pl.when(s == 0)
    def _():
        sre_ref[...] = h0re_ref[...]
        sim_ref[...] = h0im_ref[...]

    ua = ua_ref[...]
    ua_bf = ua.astype(BF16)
    nhalf = W_A // MXU_DIM
    hw = W_S // nhalf
    for hf in range(nhalf):
        lhs = ua_bf[:, hf * MXU_DIM:(hf + 1) * MXU_DIM]
        bre_s[:, hf * hw:(hf + 1) * hw] = _dot(lhs, wbre_ref[hf])
        bim_s[:, hf * hw:(hf + 1) * hw] = _dot(lhs, wbim_ref[hf])

    for cb in range(W_S // S5_COLS):
        cols = slice(cb * S5_COLS, (cb + 1) * S5_COLS)
        ar = jnp.broadcast_to(are_ref[:, cols], (SUBLANES, S5_COLS))
        ai = jnp.broadcast_to(aim_ref[:, cols], (SUBLANES, S5_COLS))

        def step(t, carry, cols=cols, ar=ar, ai=ai):
            hr, hi = carry
            r0 = pl.multiple_of(t * SUBLANES, SUBLANES)
            nr = ar * hr - ai * hi + bre_s[pl.ds(r0, SUBLANES), cols]
            ni = ar * hi + ai * hr + bim_s[pl.ds(r0, SUBLANES), cols]
            bre_s[pl.ds(r0, SUBLANES), cols] = nr
            bim_s[pl.ds(r0, SUBLANES), cols] = ni
            return nr, ni

        hr, hi = lax.fori_loop(0, ts, step, (sre_ref[:, cols], sim_ref[:, cols]), unroll=4)
        sre_ref[:, cols] = hr
        sim_ref[:, cols] = hi

    ys = []
    for hf in range(nhalf):
        hre = bre_s[:, hf * hw:(hf + 1) * hw].astype(BF16)
        him = bim_s[:, hf * hw:(hf + 1) * hw].astype(BF16)
        ys.append(_dot(hre, wcre_ref[hf]) + _dot(him, wcim_ref[hf]))
    y = jnp.concatenate(ys, axis=-1) + d_ref[...] * ua
    y = jax.nn.gelu(y, approximate=True)
    y = y * jax.nn.sigmoid(_dot(y.astype(BF16), wglu_ref[...]) + bglu_ref[...])
    ga_ref[...] = y * jax.nn.silu(za_ref[...])


def s5_mixer(u3, h0re, h0im, w, ts):
    nbg, rows, _ = u3.shape
    rs = ts * SUBLANES
    nsteps = rows // rs
    st_spec = pl.BlockSpec((None, SUBLANES, W_S), lambda g, s: (g, 0, 0))
    nhalf = W_A // MXU_DIM
    hw = W_S // nhalf
    return pl.pallas_call(
        functools.partial(_s5_kernel, ts=ts),
        grid=(nbg, nsteps),
        in_specs=[pl.BlockSpec((None, rs, W_A), lambda g, s: (g, s, 0)),
                  pl.BlockSpec((None, rs, W_A), lambda g, s: (g, s, 1)),
                  st_spec, st_spec,
                  _const_spec((1, W_S)), _const_spec((1, W_S)),
                  _const_spec((nhalf, MXU_DIM, hw)), _const_spec((nhalf, MXU_DIM, hw)),
                  _const_spec((nhalf, hw, MXU_DIM)), _const_spec((nhalf, hw, MXU_DIM)),
                  _const_spec((1, W_A)), _const_spec((W_A, W_A)), _const_spec((1, W_A))],
        out_specs=[pl.BlockSpec((None, rs, W_A), lambda g, s: (g, s, 0)), st_spec, st_spec],
        out_shape=[jax.ShapeDtypeStruct((nbg, rows, W_A), F32),
                   jax.ShapeDtypeStruct((nbg, SUBLANES, W_S), F32),
                   jax.ShapeDtypeStruct((nbg, SUBLANES, W_S), F32)],
        scratch_shapes=[pltpu.VMEM((rs, W_S), F32), pltpu.VMEM((rs, W_S), F32)],
        compiler_params=_cparams(("parallel", "arbitrary")),
        name="s5_mixer",
    )(u3, u3, h0re, h0im, w["s5_are"], w["s5_aim"], w["s5_wbre"], w["s5_wbim"],
      w["s5_wcre"], w["s5_wcim"], w["s5_d"], w["s5_wglu"], w["s5_bglu"])


def _conv_step(c, x_ref, buf0_ref, xpad, cw_ref, cb_ref, buf1_ref, rc):
    @pl.when(c == 0)
    def _():
        xpad[0:HIST, :] = buf0_ref[...]

    @pl.when(c > 0)
    def _():
        xpad[0:HIST, :] = xpad[rc:rc + HIST, :]

    xpad[HIST:HIST + rc, :] = x_ref[...]
    buf1_ref[...] = xpad[rc:rc + HIST, :]
    out = cb_ref[...]
    for tap in range(CONV_K):
        out = out + xpad[tap * SUBLANES:tap * SUBLANES + rc, :] * cw_ref[tap:tap + 1, :]
    return out


def _mlstm_kernel(xb_ref, zb_ref, buf0_ref, c0_hbm, n0_ref, m0_ref, cw_ref, cb_ref,
                  wq_ref, wk_ref, wv_ref, wif_ref, bif_ref, nw_ref, skip_ref,
                  gb_ref, c1_hbm, n1_ref, m1_ref, buf1_ref,
                  xpad, xc_s, q_s, k_s, v_s, g_s, h_s, c_s, sem, *, tc):
    g = pl.program_id(0)
    c = pl.program_id(1)
    rc = tc * SUBLANES

    def state_copy(h, to_hbm):
        hbm = (c1_hbm if to_hbm else c0_hbm).at[pl.ds(g * SUBLANES, SUBLANES), h]
        src, dst = (c_s.at[h], hbm) if to_hbm else (hbm, c_s.at[h])
        return pltpu.make_async_copy(src, dst, sem.at[h])

    @pl.when(c == 0)
    def _():
        for h in range(H_B):
            state_copy(h, False).start()
        n1_ref[...] = n0_ref[...]
        m1_ref[...] = m0_ref[...]
        for h in range(H_B):
            state_copy(h, False).wait()

    xc = jax.nn.silu(_conv_step(c, xb_ref, buf0_ref, xpad, cw_ref, cb_ref, buf1_ref, rc))
    xc_s[...] = xc
    xc_bf = xc.astype(BF16)
    xb_bf = xb_ref[...].astype(BF16)
    gates = bif_ref[...]
    for h in range(H_B):
        cols = slice(h * DH_B, (h + 1) * DH_B)
        for i, (src, w_ref, dst) in enumerate(((xc_bf, wq_ref, q_s), (xc_bf, wk_ref, k_s), (xb_bf, wv_ref, v_s))):
            r = _dot(src[:, cols], w_ref[h])
            _put_cols(dst, h * DH_B, r)
            gates = gates + _dot(r.astype(BF16), wif_ref[i * W_B + h * DH_B:i * W_B + (h + 1) * DH_B, :])
    lane = lax.broadcasted_iota(jnp.int32, (rc, LANES), 1)
    gates = jnp.where(lane < H_B, gates, -_softplus(-gates))
    is_f = lax.broadcasted_iota(jnp.int32, (SUBLANES, LANES), 1) >= H_B
    run = jnp.zeros((SUBLANES, LANES), F32)
    for t in range(tc):
        cur = gates[t * SUBLANES:(t + 1) * SUBLANES, :]
        run = jnp.where(is_f, run + cur, cur)
        g_s[0, t * SUBLANES:(t + 1) * SUBLANES, :] = run

    ri = lax.broadcasted_iota(jnp.int32, (tc, tc), 0)
    ci = lax.broadcasted_iota(jnp.int32, (tc, tc), 1)
    causal = ri >= ci
    eye = ri == ci
    m_lane = lax.broadcasted_iota(jnp.int32, (SUBLANES, 1, LANES), 2)
    kscale = DH_B ** -0.5

    gt = _get_batches(g_s, tc, 0, LANES)
    m_old = m1_ref[...]
    m_new = m_old
    for h in range(H_B):
        cols = slice(h * DH_B, (h + 1) * DH_B)
        ig = gt[:, :, h:h + 1]
        bc = gt[:, :, H_B + h:H_B + h + 1]
        a_row = jnp.sum(jnp.where(eye, ig - bc, 0.0), axis=1, keepdims=True)
        dlog = jnp.where(causal, bc + a_row, -jnp.inf)
        inter = bc + m_old[:, :, h:h + 1]
        m = jnp.maximum(inter, jnp.max(dlog, axis=2, keepdims=True))
        p = jnp.exp(dlog - m)
        q = _get_batches(q_s, tc, h * DH_B, DH_B)
        k = _get_batches(k_s, tc, h * DH_B, DH_B) * kscale
        v_bf = _get_batches(v_s, tc, h * DH_B, DH_B).astype(BF16)
        q_bf = q.astype(BF16)
        sc = _bdot_nt(q_bf, k.astype(BF16)) * p
        w_inter = jnp.exp(inter - m)
        c0 = c_s[h]
        n0 = n1_ref[:, :, cols]
        num = _bdot(sc.astype(BF16), v_bf) + w_inter * _bdot(q_bf, c0.astype(BF16))
        den = jnp.sum(sc, axis=2, keepdims=True) + w_inter * jnp.sum(q * n0, axis=2, keepdims=True)
        hv = num / jnp.maximum(jnp.abs(den), jnp.exp(-m))
        _put_batches(h_s, tc, h * DH_B, _head_norm(hv))
        m_end = m[:, tc - 1:tc, :]
        w_end = jnp.exp(bc[:, tc - 1:tc, :] - bc + ig - m_end)
        decay = jnp.exp(inter[:, tc - 1:tc, :] - m_end)
        kw = k * w_end
        c_s[h] = decay * c0 + _bdot_tn(kw.astype(BF16), v_bf)
        n1_ref[:, :, cols] = decay * n0 + jnp.sum(kw, axis=1, keepdims=True)
        m_new = jnp.where(m_lane == h, m_end, m_new)
    m1_ref[...] = m_new

    y = _get_all(h_s) * nw_ref[...] + skip_ref[...] * xc_s[...]
    gb_ref[...] = y * jax.nn.silu(zb_ref[...])

    @pl.when(c == pl.num_programs(1) - 1)
    def _():
        for h in range(H_B):
            state_copy(h, True).start()
        for h in range(H_B):
            state_copy(h, True).wait()


def mlstm_mixer(u3, buf0, c0, n0, m0, w, tc):
    nbg, rows, _ = u3.shape
    rc = tc * SUBLANES
    nch = rows // rc
    grp = lambda shape: pl.BlockSpec((None,) + shape, lambda g, c: (g, 0, 0))
    per_b = lambda width: pl.BlockSpec((SUBLANES, 1, width), lambda g, c: (g, 0, 0))
    blocked = pltpu.VMEM((W_B // LANES, rc, LANES), F32)
    return pl.pallas_call(
        functools.partial(_mlstm_kernel, tc=tc),
        grid=(nbg, nch),
        in_specs=[pl.BlockSpec((None, rc, W_B), lambda g, c: (g, c, 1)),
                  pl.BlockSpec((None, rc, W_B), lambda g, c: (g, c, 2)),
                  grp((HIST, W_B)),
                  pl.BlockSpec(memory_space=pl.ANY),
                  per_b(W_B), per_b(LANES),
                  _const_spec((CONV_K, W_B)), _const_spec((1, W_B)),
                  _const_spec((H_B, DH_B, DH_B)), _const_spec((H_B, DH_B, DH_B)), _const_spec((H_B, DH_B, DH_B)),
                  _const_spec((3 * W_B, LANES)), _const_spec((1, LANES)),
                  _const_spec((1, W_B)), _const_spec((1, W_B))],
        out_specs=[pl.BlockSpec((None, rc, W_B), lambda g, c: (g, c, 0)),
                   pl.BlockSpec(memory_space=pl.ANY),
                   per_b(W_B), per_b(LANES), grp((HIST, W_B))],
        out_shape=[jax.ShapeDtypeStruct((nbg, rows, W_B), F32),
                   jax.ShapeDtypeStruct(c0.shape, F32),
                   jax.ShapeDtypeStruct((nbg * SUBLANES, 1, W_B), F32),
                   jax.ShapeDtypeStruct((nbg * SUBLANES, 1, LANES), F32),
                   jax.ShapeDtypeStruct((nbg, HIST, W_B), F32)],
        scratch_shapes=[pltpu.VMEM((rc + HIST, W_B), F32),
                        pltpu.VMEM((rc, W_B), F32), blocked, blocked, blocked,
                        pltpu.VMEM((1, rc, LANES), F32), blocked,
                        pltpu.VMEM((H_B, SUBLANES, DH_B, DH_B), F32),
                        pltpu.SemaphoreType.DMA((H_B,))],
        compiler_params=_cparams(("parallel", "arbitrary")),
        name="mlstm_mixer",
    )(u3, u3, buf0, c0, n0, m0, w["ml_cw"], w["ml_cb"], w["ml_wq"], w["ml_wk"], w["ml_wv"],
      w["ml_wif"], w["ml_bif"], w["ml_nw"], w["ml_skip"])


def _ret_log_decay(h):
    return float(np.log1p(-np.exp2(-5.0 - h)))


def _retention_kernel(q_ref, k_ref, v_ref, zc_ref, cos_ref, sin_ref, s0_hbm, nw_ref,
                      gc_ref, s1_hbm, q_s, k_s, v_s, o_s, st_s, sem, *, tc):
    g = pl.program_id(0)
    c = pl.program_id(1)

    def state_copy(h, to_hbm):
        hbm = (s1_hbm if to_hbm else s0_hbm).at[pl.ds(g * SUBLANES, SUBLANES), h]
        src, dst = (st_s.at[h], hbm) if to_hbm else (hbm, st_s.at[h])
        return pltpu.make_async_copy(src, dst, sem.at[h])

    @pl.when(c == 0)
    def _():
        for h in range(H_C):
            state_copy(h, False).start()
        for h in range(H_C):
            state_copy(h, False).wait()

    cos, sin = cos_ref[...], sin_ref[...]
    _put_cols(v_s, 0, v_ref[...])
    for h in range(H_C):
        cols = slice(h * DK_C, (h + 1) * DK_C)
        for src, dst, scale in ((q_ref, q_s, None), (k_ref, k_s, DK_C ** -0.5)):
            x = src[:, cols]
            r = x * cos + pltpu.roll(x, DK_C // 2, 1) * sin
            dst[h] = r if scale is None else r * scale

    ri = lax.broadcasted_iota(jnp.int32, (tc, tc), 0)
    ci = lax.broadcasted_iota(jnp.int32, (tc, tc), 1)
    diff = (ri - ci).astype(F32)
    tcol = lax.broadcasted_iota(jnp.int32, (tc, 1), 0).astype(F32)
    dmask, xi, zeta, gfull = [], [], [], []
    for h in range(H_C):
        lg = _ret_log_decay(h)
        dmask.append(jnp.where(diff >= 0, jnp.exp(lg * jnp.maximum(diff, 0.0)), 0.0))
        xi.append(jnp.exp(lg * (tcol + 1.0)))
        zeta.append(jnp.exp(lg * (tc - 1.0 - tcol)))
        gfull.append(float(np.exp(np.float32(lg) * np.float32(tc))))

    for h in range(H_C):
        q_bf = _get_batches(q_s, tc, h * DK_C, DK_C).astype(BF16)
        k = _get_batches(k_s, tc, h * DK_C, DK_C)
        v_bf = _get_batches(v_s, tc, h * DV_C, DV_C).astype(BF16)
        s0 = st_s[h]
        inner = _bdot_nt(q_bf, k.astype(BF16)) * dmask[h]
        o = _bdot(inner.astype(BF16), v_bf) + xi[h] * _bdot(q_bf, s0.astype(BF16))
        _put_batches(o_s, tc, h * DV_C, _head_norm(o))
        st_s[h] = gfull[h] * s0 + _bdot_tn((k * zeta[h]).astype(BF16), v_bf)

    gc_ref[...] = _get_all(o_s) * nw_ref[...] * jax.nn.silu(zc_ref[...])

    @pl.when(c == pl.num_programs(1) - 1)
    def _():
        for h in range(H_C):
            state_copy(h, True).start()
        for h in range(H_C):
            state_copy(h, True).wait()


def retention_mixer(u3, cos, sin, s0, w, tc):
    nbg, rows, _ = u3.shape
    rc = tc * SUBLANES
    nch = rows // rc
    return pl.pallas_call(
        functools.partial(_retention_kernel, tc=tc),
        grid=(nbg, nch),
        in_specs=[pl.BlockSpec((None, rc, QK_C), lambda g, c: (g, c, 0)),
                  pl.BlockSpec((None, rc, QK_C), lambda g, c: (g, c, 1)),
                  pl.BlockSpec((None, rc, W_C), lambda g, c: (g, c, 1)),
                  pl.BlockSpec((None, rc, W_C), lambda g, c: (g, c, 2)),
                  pl.BlockSpec((None, rc, DK_C), lambda g, c: (g, c, 0)),
                  pl.BlockSpec((None, rc, DK_C), lambda g, c: (g, c, 0)),
                  pl.BlockSpec(memory_space=pl.ANY),
                  _const_spec((1, W_C))],
        out_specs=[pl.BlockSpec((None, rc, W_C), lambda g, c: (g, c, 0)),
                   pl.BlockSpec(memory_space=pl.ANY)],
        out_shape=[jax.ShapeDtypeStruct((nbg, rows, W_C), F32),
                   jax.ShapeDtypeStruct(s0.shape, F32)],
        scratch_shapes=[pltpu.VMEM((QK_C // LANES, rc, LANES), F32), pltpu.VMEM((QK_C // LANES, rc, LANES), F32),
                        pltpu.VMEM((W_C // LANES, rc, LANES), F32), pltpu.VMEM((W_C // LANES, rc, LANES), F32),
                        pltpu.VMEM((H_C, SUBLANES, DK_C, DV_C), F32),
                        pltpu.SemaphoreType.DMA((H_C,))],
        compiler_params=_cparams(("parallel", "arbitrary")),
        name="retention_mixer",
    )(u3, u3, u3, u3, cos, sin, s0, w["ret_nw"])


def _rglru_kernel(xd_ref, zd_ref, buf0_ref, h0_ref, cw_ref, cb_ref, wa_ref, ba_ref, wx_ref, bx_ref, lam_ref,
                  gd_ref, h1_ref, buf1_ref, xpad, a_s, b_s, *, tc):
    c = pl.program_id(1)
    rc = tc * SUBLANES

    @pl.when(c == 0)
    def _():
        h1_ref[...] = h0_ref[...]

    xc = _conv_step(c, xd_ref, buf0_ref, xpad, cw_ref, cb_ref, buf1_ref, rc)
    xc_bf = xc.astype(BF16)
    nblk = W_D // MXU_DIM
    rg, ig = [], []
    for j in range(nblk):
        lhs = xc_bf[:, j * MXU_DIM:(j + 1) * MXU_DIM]
        rg.append(_dot(lhs, wa_ref[j]))
        ig.append(_dot(lhs, wx_ref[j]))
    r = jax.nn.sigmoid(jnp.concatenate(rg, axis=-1) + ba_ref[...])
    i = jax.nn.sigmoid(jnp.concatenate(ig, axis=-1) + bx_ref[...])
    log_a = -LRU_C * r * _softplus(-lam_ref[...])
    a = jnp.exp(log_a)
    a_s[...] = a
    b_s[...] = jnp.sqrt(-jnp.tanh(log_a) * (a * a + 1.0)) * (i * xc)

    def step(t, h):
        r0 = pl.multiple_of(t * SUBLANES, SUBLANES)
        h = a_s[pl.ds(r0, SUBLANES), :] * h + b_s[pl.ds(r0, SUBLANES), :]
        b_s[pl.ds(r0, SUBLANES), :] = h
        return h

    h1_ref[...] = lax.fori_loop(0, tc, step, h1_ref[...], unroll=4)
    gd_ref[...] = b_s[...] * jax.nn.silu(zd_ref[...])


def rglru_mixer(u3, buf0, h0, w, tc):
    nbg, rows, _ = u3.shape
    rc = tc * SUBLANES
    nch = rows // rc
    nblk = W_D // MXU_DIM
    grp = lambda shape: pl.BlockSpec((None,) + shape, lambda g, c: (g, 0, 0))
    return pl.pallas_call(
        functools.partial(_rglru_kernel, tc=tc),
        grid=(nbg, nch),
        in_specs=[pl.BlockSpec((None, rc, W_D), lambda g, c: (g, c, 3)),
                  pl.BlockSpec((None, rc, W_D), lambda g, c: (g, c, 4)),
                  grp((HIST, W_D)), grp((SUBLANES, W_D)),
                  _const_spec((CONV_K, W_D)), _const_spec((1, W_D)),
                  _const_spec((nblk, MXU_DIM, MXU_DIM)), _const_spec((1, W_D)),
                  _const_spec((nblk, MXU_DIM, MXU_DIM)), _const_spec((1, W_D)),
                  _const_spec((1, W_D))],
        out_specs=[pl.BlockSpec((None, rc, W_D), lambda g, c: (g, c, 0)),
                   grp((SUBLANES, W_D)), grp((HIST, W_D))],
        out_shape=[jax.ShapeDtypeStruct((nbg, rows, W_D), F32),
                   jax.ShapeDtypeStruct((nbg, SUBLANES, W_D), F32),
                   jax.ShapeDtypeStruct((nbg, HIST, W_D), F32)],
        scratch_shapes=[pltpu.VMEM((rc + HIST, W_D), F32), pltpu.VMEM((rc, W_D), F32), pltpu.VMEM((rc, W_D), F32)],
        compiler_params=_cparams(("parallel", "arbitrary")),
        name="rglru_mixer",
    )(u3, u3, buf0, h0, w["lru_cw"], w["lru_cb"], w["lru_wa"], w["lru_ba"], w["lru_wx"], w["lru_bx"], w["lru_lam"])


def _to_tm(x):
    b, t, c = x.shape
    return x.reshape(b // SUBLANES, SUBLANES, t, c).swapaxes(1, 2).reshape(b // SUBLANES, t * SUBLANES, c)


def _from_tm(x, t):
    nbg, _, c = x.shape
    return x.reshape(nbg, t, SUBLANES, c).swapaxes(1, 2).reshape(nbg * SUBLANES, t, c)


def _rope_tables(pos):
    half = DK_C // 2
    inv = ROPE_BASE ** (-jnp.arange(half, dtype=F32) / half)
    ang = pos[..., None] * inv
    cos = jnp.cos(ang)
    sin = jnp.sin(ang)
    cos = jnp.concatenate([cos, cos], axis=-1)
    sin = jnp.concatenate([-sin, sin], axis=-1)
    rep = lambda a: jnp.repeat(a, SUBLANES, axis=1)
    return rep(cos), rep(sin)


def _prep_weights(p):
    w = {}
    w["norm0"] = p["norm_w"][0][None]
    w["norm1"] = p["norm_w"][1][None]
    w["final_norm"] = p["final_norm_w"][None]
    w["ev_w_in"] = p["ev_w_in"][0].astype(BF16)
    w["ev_w_out_a"] = p["ev_w_out"][0][:W_A].astype(BF16)
    w["ev_w_out_b"] = p["ev_w_out"][0][W_A:].astype(BF16)
    are, aim, bbre, bbim = s5_prep(p["s5_lambda_re"][0], p["s5_lambda_im"][0], p["s5_log_dt"][0],
                                   p["s5_b_re"][0], p["s5_b_im"][0])
    per = MXU_DIM // S5_GROUP
    w["s5_are"] = are.reshape(1, W_S)
    w["s5_aim"] = aim.reshape(1, W_S)
    w["s5_wbre"] = _block_diag(bbre, per).astype(BF16)
    w["s5_wbim"] = _block_diag(bbim, per).astype(BF16)
    w["s5_wcre"] = _block_diag(jnp.swapaxes(p["s5_c_re"][0], 1, 2), per).astype(BF16)
    w["s5_wcim"] = _block_diag(-jnp.swapaxes(p["s5_c_im"][0], 1, 2), per).astype(BF16)
    w["s5_d"] = p["s5_d"][0][None]
    w["s5_wglu"] = p["s5_w_glu"][0].astype(BF16)
    w["s5_bglu"] = p["s5_b_glu"][0][None]
    w["ml_cw"] = p["ml_conv_w"][0]
    w["ml_cb"] = p["ml_conv_b"][0][None]
    w["ml_wq"] = p["ml_wq"][0].astype(BF16)
    w["ml_wk"] = p["ml_wk"][0].astype(BF16)
    w["ml_wv"] = p["ml_wv"][0].astype(BF16)
    w["ml_wif"] = jnp.pad(p["ml_w_if"][0], ((0, 0), (0, LANES - 2 * H_B))).astype(BF16)
    w["ml_bif"] = jnp.pad(p["ml_b_if"][0], (0, LANES - 2 * H_B))[None]
    w["ml_nw"] = p["ml_norm_w"][0][None]
    w["ml_skip"] = p["ml_skip"][0][None]
    w["od_w_in"] = p["od_w_in"][0].astype(BF16)
    w["od_w_out_c"] = p["od_w_out"][0][:W_C].astype(BF16)
    w["od_w_out_d"] = p["od_w_out"][0][W_C:].astype(BF16)
    w["ret_nw"] = p["ret_norm_w"][0][None]
    w["lru_cw"] = p["lru_conv_w"][0]
    w["lru_cb"] = p["lru_conv_b"][0][None]
    w["lru_wa"] = _block_diag(p["lru_w_a"][0], MXU_DIM // BD_D).astype(BF16)
    w["lru_ba"] = p["lru_b_a"][0][None]
    w["lru_wx"] = _block_diag(p["lru_w_x"][0], MXU_DIM // BD_D).astype(BF16)
    w["lru_bx"] = p["lru_b_x"][0][None]
    w["lru_lam"] = p["lru_lambda"][0][None]
    return w


def _conv_to_tm(buf):
    return _to_tm(buf)


def _trunk(x3, pos, st, w, tc):
    nbg, rows, d = x3.shape
    flat = lambda a: a.reshape(nbg * rows, a.shape[-1])
    unflat = lambda a: a.reshape(nbg, rows, a.shape[-1])
    new = {}
    u = unflat(proj_in(flat(x3), w["norm0"], w["ev_w_in"]))
    ga, new["s5_re"], new["s5_im"] = s5_mixer(u, st["s5_re"], st["s5_im"], w, tc)
    gb, new["ml_c"], new["ml_n"], new["ml_m"], new["ml_conv"] = mlstm_mixer(
        u, st["ml_conv"], st["ml_c"], st["ml_n"], st["ml_m"], w, tc)
    h1 = proj_out(flat(x3), flat(ga), flat(gb), w["ev_w_out_a"], w["ev_w_out_b"], w["final_norm"], False)
    u = unflat(proj_in(h1, w["norm1"], w["od_w_in"]))
    cos, sin = _rope_tables(pos)
    gc, new["ret"] = retention_mixer(u, cos, sin, st["ret"], w, tc)
    gd, new["lru_h"], new["lru_conv"] = rglru_mixer(u, st["lru_conv"], st["lru_h"], w, tc)
    y = proj_out(h1, flat(gc), flat(gd), w["od_w_out_c"], w["od_w_out_d"], w["final_norm"], True)
    return unflat(y), new


def _states_in(s5_re, s5_im, ml_c, ml_n, ml_m, ml_conv, ret, lru_h, lru_conv):
    b = s5_re.shape[0]
    nbg = b // SUBLANES
    return dict(
        s5_re=s5_re.reshape(nbg, SUBLANES, W_S),
        s5_im=s5_im.reshape(nbg, SUBLANES, W_S),
        ml_c=ml_c,
        ml_n=ml_n.reshape(b, 1, W_B),
        ml_m=jnp.pad(ml_m, ((0, 0), (0, LANES - H_B))).reshape(b, 1, LANES),
        ml_conv=_conv_to_tm(ml_conv),
        ret=ret,
        lru_h=lru_h.reshape(nbg, SUBLANES, W_D),
        lru_conv=_conv_to_tm(lru_conv),
    )


def _states_out(st, sel=None):
    nbg = st["s5_re"].shape[0]
    b = nbg * SUBLANES
    out = dict(
        s5_re=st["s5_re"].reshape(b, G_A, S5_STATE),
        s5_im=st["s5_im"].reshape(b, G_A, S5_STATE),
        ml_c=st["ml_c"].reshape(b, H_B, DH_B, DH_B),
        ml_n=st["ml_n"].reshape(b, H_B, DH_B),
        ml_m=st["ml_m"].reshape(b, LANES)[:, :H_B],
        ml_conv=_from_tm(st["ml_conv"], CONV_K - 1),
        ret=st["ret"].reshape(b, H_C, DK_C, DV_C),
        lru_h=st["lru_h"].reshape(b, W_D),
        lru_conv=_from_tm(st["lru_conv"], CONV_K - 1),
    )
    if sel is not None:
        out = {k: v[sel] for k, v in out.items()}
    return out


_STATE_ORDER = ("s5_re", "s5_im", "ml_c", "ml_n", "ml_m", "ml_conv", "ret", "lru_h", "lru_conv")


def kernel(x_prompt, x_sample, state_s5_re, state_s5_im, state_ml_c, state_ml_n, state_ml_m, state_ml_conv, state_ret, state_lru_h, state_lru_conv, meta, norm_w, final_norm_w, ev_w_in, ev_w_out, s5_lambda_re, s5_lambda_im, s5_log_dt, s5_b_re, s5_b_im, s5_c_re, s5_c_im, s5_d, s5_w_glu, s5_b_glu, ml_conv_w, ml_conv_b, ml_wq, ml_wk, ml_wv, ml_w_if, ml_b_if, ml_norm_w, ml_skip, od_w_in, od_w_out, ret_norm_w, lru_conv_w, lru_conv_b, lru_w_a, lru_b_a, lru_w_x, lru_b_x, lru_lambda):
    p = dict(norm_w=norm_w, final_norm_w=final_norm_w, ev_w_in=ev_w_in, ev_w_out=ev_w_out,
             s5_lambda_re=s5_lambda_re, s5_lambda_im=s5_lambda_im, s5_log_dt=s5_log_dt,
             s5_b_re=s5_b_re, s5_b_im=s5_b_im, s5_c_re=s5_c_re, s5_c_im=s5_c_im, s5_d=s5_d,
             s5_w_glu=s5_w_glu, s5_b_glu=s5_b_glu, ml_conv_w=ml_conv_w, ml_conv_b=ml_conv_b,
             ml_wq=ml_wq, ml_wk=ml_wk, ml_wv=ml_wv, ml_w_if=ml_w_if, ml_b_if=ml_b_if,
             ml_norm_w=ml_norm_w, ml_skip=ml_skip, od_w_in=od_w_in, od_w_out=od_w_out,
             ret_norm_w=ret_norm_w, lru_conv_w=lru_conv_w, lru_conv_b=lru_conv_b,
             lru_w_a=lru_w_a, lru_b_a=lru_b_a, lru_w_x=lru_w_x, lru_b_x=lru_b_x, lru_lambda=lru_lambda)
    assert ev_w_in.shape[0] == 1 and od_w_in.shape[0] == 1, "two-layer trunk"
    w = _prep_weights(p)
    bp, tp, _ = x_prompt.shape
    bs, ts, _ = x_sample.shape
    assert ts == N_META and bp % SUBLANES == 0 and bs % SUBLANES == 0 and tp % CHUNK == 0

    meta_b = jnp.broadcast_to(meta[None], (bp, N_META, D_MODEL))
    x_short = _to_tm(jnp.concatenate([x_sample, meta_b], axis=0))
    given = (state_s5_re[0], state_s5_im[0], state_ml_c[0], state_ml_n[0], state_ml_m[0], state_ml_conv[0],
             state_ret[0], state_lru_h[0], state_lru_conv[0])
    both = [jnp.concatenate([s, jnp.zeros((bp,) + s.shape[1:], s.dtype)], axis=0) for s in given]
    tpos = jnp.arange(N_META, dtype=F32)
    pos_short = jnp.concatenate([jnp.broadcast_to((N_META + PAST_LEN) + tpos, (bs // SUBLANES, N_META)),
                                 jnp.broadcast_to(tpos, (bp // SUBLANES, N_META))], axis=0)
    y_short, st_short = _trunk(x_short, pos_short, _states_in(*both), w, N_META)
    y_sample = _from_tm(y_short, N_META)[:bs]
    new_s = _states_out(st_short, slice(0, bs))

    nbs = bs // SUBLANES
    st_meta = {}
    for name, val in st_short.items():
        per_group = val.shape[0] // (x_short.shape[0])
        st_meta[name] = val[nbs * per_group:]
    pos_long = jnp.broadcast_to(N_META + jnp.arange(tp, dtype=F32), (bp // SUBLANES, tp))
    y_long, st_long = _trunk(_to_tm(x_prompt), pos_long, st_meta, w, CHUNK)
    y_prompt = _from_tm(y_long, tp)
    new_p = _states_out(st_long)

    return ((y_prompt, y_sample)
            + tuple(new_p[k][None] for k in _STATE_ORDER)
            + tuple(new_s[k][None] for k in _STATE_ORDER))
```

```python
import functools
import math

import numpy as np
import jax
import jax.numpy as jnp
from jax import lax
from jax.experimental import pallas as pl
from jax.experimental.pallas import tpu as pltpu

F32 = jnp.float32
BF16 = jnp.bfloat16

SUBLANES = 8
LANES = 128
MXU_DIM = 256
VMEM_LIMIT = 56 * 1024 * 1024

D_MODEL = 1024
CHUNK = 64
N_META = 16
PAST_LEN = 2048
EPS = 1e-6
CONV_K = 4
S5_GROUP = 16
S5_STATE = 64
W_A = D_MODEL // 2
G_A = W_A // S5_GROUP
W_S = G_A * S5_STATE
W_B = D_MODEL
H_B = 4
DH_B = W_B // H_B
W_C = D_MODEL
H_C = 4
DV_C = W_C // H_C
DK_C = DV_C // 2
QK_C = H_C * DK_C
ROPE_BASE = 10000.0
W_D = D_MODEL
H_D = 8
BD_D = W_D // H_D
LRU_C = 8.0
EV_IN = 2 * W_A + 2 * W_B
OD_IN = 2 * QK_C + 2 * W_C + 2 * W_D
HIST = (CONV_K - 1) * SUBLANES

ROW_TILE = 512
S5_COLS = 1024


def _cparams(sem):
    return pltpu.CompilerParams(dimension_semantics=sem, vmem_limit_bytes=VMEM_LIMIT)


def _const_spec(shape):
    nd = len(shape)
    return pl.BlockSpec(shape, lambda *_: (0,) * nd, pipeline_mode=pl.Buffered(1))


def _dot(a, b):
    return jnp.dot(a, b, preferred_element_type=F32)


def _dot_nt(a, b):
    return lax.dot_general(a, b, (((1,), (1,)), ((), ())), preferred_element_type=F32)


def _dot_tn(a, b):
    return lax.dot_general(a, b, (((0,), (0,)), ((), ())), preferred_element_type=F32)


def _rms(x, g):
    return x * lax.rsqrt(jnp.mean(x * x, axis=-1, keepdims=True) + EPS) * g


def _softplus(x):
    return jnp.maximum(x, 0.0) + jnp.log1p(jnp.exp(-jnp.abs(x)))


def _sigmoid(x):
    return 0.5 * jnp.tanh(0.5 * x) + 0.5


def _silu(x):
    return x * _sigmoid(x)


def _put_cols(dst3, col0, val):
    for j in range(val.shape[1] // LANES):
        dst3[col0 // LANES + j] = val[:, j * LANES:(j + 1) * LANES]


def _get_all(src3):
    return jnp.concatenate([src3[j] for j in range(src3.shape[0])], axis=-1)


def _get_batch(src3, b, tc, col0, width):
    parts = [src3[col0 // LANES + j, pl.ds(b, tc, stride=SUBLANES), :] for j in range(width // LANES)]
    return parts[0] if len(parts) == 1 else jnp.concatenate(parts, axis=-1)


def _put_batch(dst3, b, tc, col0, val):
    for j in range(val.shape[1] // LANES):
        dst3[col0 // LANES + j, pl.ds(b, tc, stride=SUBLANES), :] = val[:, j * LANES:(j + 1) * LANES]


def _get_batches(src3, tc, col0, width):
    return jnp.stack([_get_batch(src3, b, tc, col0, width) for b in range(SUBLANES)], axis=0)


def _put_batches(dst3, tc, col0, val):
    for b in range(SUBLANES):
        _put_batch(dst3, b, tc, col0, val[b])


def _bdot(a, b):
    return lax.dot_general(a, b, (((2,), (1,)), ((0,), (0,))), preferred_element_type=F32)


def _bdot_nt(a, b):
    return lax.dot_general(a, b, (((2,), (2,)), ((0,), (0,))), preferred_element_type=F32)


def _bdot_tn(a, b):
    return lax.dot_general(a, b, (((1,), (1,)), ((0,), (0,))), preferred_element_type=F32)


def _head_norm(h):
    mu = jnp.mean(h, axis=-1, keepdims=True)
    d = h - mu
    var = jnp.mean(d * d, axis=-1, keepdims=True)
    return d * lax.rsqrt(var + EPS)


def _load_bm_as_tm(x_ref, xs):
    tt = x_ref.shape[1]
    for b in range(SUBLANES):
        _put_batch(xs, b, tt, 0, x_ref[b])
    return _get_all(xs)


def _act_spec(bm, tt, width):
    if bm:
        return pl.BlockSpec((SUBLANES, tt, width), lambda g, i: (g, i, 0))
    return pl.BlockSpec((None, tt * SUBLANES, width), lambda g, i: (g, i, 0))


def _act_dims(x, bm):
    return (x.shape[0] // SUBLANES, x.shape[1]) if bm else (x.shape[0], x.shape[1] // SUBLANES)


def _proj_in_kernel(x_ref, g_ref, w_ref, o_ref, *scratch, x_bm):
    x = _load_bm_as_tm(x_ref, scratch[0]) if x_bm else x_ref[...]
    o_ref[...] = _dot(_rms(x, g_ref[...]).astype(BF16), w_ref[...])


def proj_in(x, g, w, x_bm):
    d, n = w.shape
    nbg, t = _act_dims(x, x_bm)
    tt = min(ROW_TILE // SUBLANES, t)
    rows = tt * SUBLANES
    return pl.pallas_call(
        functools.partial(_proj_in_kernel, x_bm=x_bm),
        grid=(nbg, t // tt),
        in_specs=[_act_spec(x_bm, tt, d), _const_spec((1, d)), _const_spec((d, n))],
        out_specs=_act_spec(False, tt, n),
        out_shape=jax.ShapeDtypeStruct((nbg, t * SUBLANES, n), F32),
        scratch_shapes=[pltpu.VMEM((d // LANES, rows, LANES), F32)] if x_bm else [],
        compiler_params=_cparams(("parallel", "parallel")),
        name="proj_in",
    )(x, g, w)


def _proj_out_kernel(h_ref, a_ref, b_ref, wa_ref, wb_ref, g_ref, o_ref, *scratch, h_bm, final):
    h = _load_bm_as_tm(h_ref, scratch[0]) if h_bm else h_ref[...]
    y = h + _dot(a_ref[...], wa_ref[...]) + _dot(b_ref[...], wb_ref[...])
    if final:
        ys = scratch[-1]
        _put_cols(ys, 0, _rms(y, g_ref[...]))
        tt = o_ref.shape[1]
        for b in range(SUBLANES):
            o_ref[b] = _get_batch(ys, b, tt, 0, o_ref.shape[2])
    else:
        o_ref[...] = y


def proj_out(h, a3, b3, wa, wb, g, h_bm, final):
    ka, d = wa.shape
    kb = wb.shape[0]
    nbg, t = _act_dims(h, h_bm)
    tt = min(ROW_TILE // SUBLANES, t)
    rows = tt * SUBLANES
    blocked = pltpu.VMEM((d // LANES, rows, LANES), F32)
    out_shape = (nbg * SUBLANES, t, d) if final else (nbg, t * SUBLANES, d)
    return pl.pallas_call(
        functools.partial(_proj_out_kernel, h_bm=h_bm, final=final),
        grid=(nbg, t // tt),
        in_specs=[_act_spec(h_bm, tt, d), _act_spec(False, tt, ka), _act_spec(False, tt, kb),
                  _const_spec((ka, d)), _const_spec((kb, d)), _const_spec((1, d))],
        out_specs=_act_spec(final, tt, d),
        out_shape=jax.ShapeDtypeStruct(out_shape, F32),
        scratch_shapes=[blocked] * (int(h_bm) + int(final)),
        compiler_params=_cparams(("parallel", "parallel")),
        name="proj_out",
    )(h, a3, b3, wa, wb, g)


def _s5_prep_kernel(lre_ref, lim_ref, ldt_ref, bre_ref, bim_ref, are_ref, aim_ref, bbre_ref, bbim_ref):
    lre, lim = lre_ref[...], lim_ref[...]
    dt = jnp.exp(ldt_ref[...])
    mag = jnp.exp(lre * dt)
    ab_re = mag * jnp.cos(lim * dt)
    ab_im = mag * jnp.sin(lim * dt)
    den = lre * lre + lim * lim
    nr = ab_re - 1.0
    k_re = (nr * lre + ab_im * lim) / den
    k_im = (ab_im * lre - nr * lim) / den
    bre, bim = bre_ref[...], bim_ref[...]
    are_ref[...] = ab_re
    aim_ref[...] = ab_im
    bbre_ref[...] = k_re * bre - k_im * bim
    bbim_ref[...] = k_re * bim + k_im * bre


def s5_prep(lam_re, lam_im, log_dt, b_re, b_im):
    rep = lambda a: jnp.repeat(a, S5_GROUP, axis=0)
    rows = G_A * S5_GROUP
    ldt = jnp.broadcast_to(rep(log_dt[:, None]), (rows, S5_STATE))
    tr = lambda b: jnp.swapaxes(b, 1, 2).reshape(rows, S5_STATE)
    outs = pl.pallas_call(
        _s5_prep_kernel,
        out_shape=[jax.ShapeDtypeStruct((rows, S5_STATE), F32)] * 4,
        name="s5_prep",
    )(rep(lam_re), rep(lam_im), ldt, tr(b_re), tr(b_im))
    are, aim, bbre, bbim = (o.reshape(G_A, S5_GROUP, S5_STATE) for o in outs)
    return are[:, 0], aim[:, 0], bbre, bbim


def _block_diag(blocks, per):
    n, r, c = blocks.shape
    b = blocks.reshape(n // per, per, r, c)
    eye = jnp.eye(per, dtype=blocks.dtype)
    return jnp.einsum("hgrc,gk->hgrkc", b, eye).reshape(n // per, per * r, per * c)


def _s5_kernel(ua_ref, za_ref, h0re_ref, h0im_ref, are_ref, aim_ref, wbre_ref, wbim_ref,
               wcre_ref, wcim_ref, d_ref, wglu_ref, bglu_ref,
               ga_ref, sre_ref, sim_ref, bre_s, bim_s, *, ts):
    s = pl.program_id(1)

    @pl.when(s == 0)
    def _():
        sre_ref[...] = h0re_ref[...]
        sim_ref[...] = h0im_ref[...]

    ua = ua_ref[...]
    ua_bf = ua.astype(BF16)
    nhalf = W_A // MXU_DIM
    hw = W_S // nhalf
    for hf in range(nhalf):
        lhs = ua_bf[:, hf * MXU_DIM:(hf + 1) * MXU_DIM]
        bre_s[:, hf * hw:(hf + 1) * hw] = _dot(lhs, wbre_ref[hf])
        bim_s[:, hf * hw:(hf + 1) * hw] = _dot(lhs, wbim_ref[hf])

    for cb in range(W_S // S5_COLS):
        cols = slice(cb * S5_COLS, (cb + 1) * S5_COLS)
        ar = jnp.broadcast_to(are_ref[:, cols], (SUBLANES, S5_COLS))
        ai = jnp.broadcast_to(aim_ref[:, cols], (SUBLANES, S5_COLS))

        def step(t, carry, cols=cols, ar=ar, ai=ai):
            hr, hi = carry
            r0 = pl.multiple_of(t * SUBLANES, SUBLANES)
            nr = ar * hr - ai * hi + bre_s[pl.ds(r0, SUBLANES), cols]
            ni = ar * hi + ai * hr + bim_s[pl.ds(r0, SUBLANES), cols]
            bre_s[pl.ds(r0, SUBLANES), cols] = nr
            bim_s[pl.ds(r0, SUBLANES), cols] = ni
            return nr, ni

        hr, hi = lax.fori_loop(0, ts, step, (sre_ref[:, cols], sim_ref[:, cols]), unroll=4)
        sre_ref[:, cols] = hr
        sim_ref[:, cols] = hi

    ys = []
    for hf in range(nhalf):
        hre = bre_s[:, hf * hw:(hf + 1) * hw].astype(BF16)
        him = bim_s[:, hf * hw:(hf + 1) * hw].astype(BF16)
        ys.append(_dot(hre, wcre_ref[hf]) + _dot(him, wcim_ref[hf]))
    y = jnp.concatenate(ys, axis=-1) + d_ref[...] * ua
    y = jax.nn.gelu(y, approximate=True)
    y = y * _sigmoid(_dot(y.astype(BF16), wglu_ref[...]) + bglu_ref[...])
    ga_ref[...] = (y * _silu(za_ref[...])).astype(BF16)


def s5_mixer(u3, h0re, h0im, w, ts):
    nbg, rows, _ = u3.shape
    rs = ts * SUBLANES
    nsteps = rows // rs
    st_spec = pl.BlockSpec((None, SUBLANES, W_S), lambda g, s: (g, 0, 0))
    nhalf = W_A // MXU_DIM
    hw = W_S // nhalf
    return pl.pallas_call(
        functools.partial(_s5_kernel, ts=ts),
        grid=(nbg, nsteps),
        in_specs=[pl.BlockSpec((None, rs, W_A), lambda g, s: (g, s, 0)),
                  pl.BlockSpec((None, rs, W_A), lambda g, s: (g, s, 1)),
                  st_spec, st_spec,
                  _const_spec((1, W_S)), _const_spec((1, W_S)),
                  _const_spec((nhalf, MXU_DIM, hw)), _const_spec((nhalf, MXU_DIM, hw)),
                  _const_spec((nhalf, hw, MXU_DIM)), _const_spec((nhalf, hw, MXU_DIM)),
                  _const_spec((1, W_A)), _const_spec((W_A, W_A)), _const_spec((1, W_A))],
        out_specs=[pl.BlockSpec((None, rs, W_A), lambda g, s: (g, s, 0)), st_spec, st_spec],
        out_shape=[jax.ShapeDtypeStruct((nbg, rows, W_A), BF16),
                   jax.ShapeDtypeStruct((nbg, SUBLANES, W_S), F32),
                   jax.ShapeDtypeStruct((nbg, SUBLANES, W_S), F32)],
        scratch_shapes=[pltpu.VMEM((rs, W_S), F32), pltpu.VMEM((rs, W_S), F32)],
        compiler_params=_cparams(("parallel", "arbitrary")),
        name="s5_mixer",
    )(u3, u3, h0re, h0im, w["s5_are"], w["s5_aim"], w["s5_wbre"], w["s5_wbim"],
      w["s5_wcre"], w["s5_wcim"], w["s5_d"], w["s5_wglu"], w["s5_bglu"])


def _conv_step(c, x_ref, buf0_ref, xpad, cw_ref, cb_ref, buf1_ref, rc):
    @pl.when(c == 0)
    def _():
        xpad[0:HIST, :] = buf0_ref[...]

    @pl.when(c > 0)
    def _():
        xpad[0:HIST, :] = xpad[rc:rc + HIST, :]

    xpad[HIST:HIST + rc, :] = x_ref[...]
    buf1_ref[...] = xpad[rc:rc + HIST, :]
    out = cb_ref[...]
    for tap in range(CONV_K):
        out = out + xpad[tap * SUBLANES:tap * SUBLANES + rc, :] * cw_ref[tap:tap + 1, :]
    return out


def _mlstm_kernel(xb_ref, zb_ref, buf0_ref, c0_hbm, n0_ref, m0_ref, cw_ref, cb_ref,
                  wq_ref, wk_ref, wv_ref, wif_ref, bif_ref, nw_ref, skip_ref,
                  gb_ref, c1_hbm, n1_ref, m1_ref, buf1_ref,
                  xpad, xc_s, q_s, k_s, v_s, g_s, h_s, c_s, sem, *, tc, n_given):
    g = pl.program_id(0)
    c = pl.program_id(1)
    rc = tc * SUBLANES

    def state_copy(h, to_hbm):
        hbm = (c1_hbm if to_hbm else c0_hbm).at[pl.ds(g * SUBLANES, SUBLANES), h]
        src, dst = (c_s.at[h], hbm) if to_hbm else (hbm, c_s.at[h])
        return pltpu.make_async_copy(src, dst, sem.at[h])

    @pl.when(c == 0)
    def _():
        n1_ref[...] = n0_ref[...]
        m1_ref[...] = m0_ref[...]

    @pl.when((c == 0) & (g < n_given))
    def _():
        for h in range(H_B):
            state_copy(h, False).start()
        for h in range(H_B):
            state_copy(h, False).wait()

    @pl.when((c == 0) & (g >= n_given))
    def _():
        c_s[...] = jnp.zeros(c_s.shape, F32)

    xc = _silu(_conv_step(c, xb_ref, buf0_ref, xpad, cw_ref, cb_ref, buf1_ref, rc))
    xc_s[...] = xc
    xc_bf = xc.astype(BF16)
    xb_bf = xb_ref[...].astype(BF16)
    gates = bif_ref[...]
    for h in range(H_B):
        cols = slice(h * DH_B, (h + 1) * DH_B)
        for i, (src, w_ref, dst) in enumerate(((xc_bf, wq_ref, q_s), (xc_bf, wk_ref, k_s), (xb_bf, wv_ref, v_s))):
            r = _dot(src[:, cols], w_ref[h])
            _put_cols(dst, h * DH_B, r)
            gates = gates + _dot(r.astype(BF16), wif_ref[i * W_B + h * DH_B:i * W_B + (h + 1) * DH_B, :])
    lane = lax.broadcasted_iota(jnp.int32, (rc, LANES), 1)
    gates = jnp.where(lane < H_B, gates, -_softplus(-gates))
    is_f = lax.broadcasted_iota(jnp.int32, (SUBLANES, LANES), 1) >= H_B
    run = jnp.zeros((SUBLANES, LANES), F32)
    for t in range(tc):
        cur = gates[t * SUBLANES:(t + 1) * SUBLANES, :]
        run = jnp.where(is_f, run + cur, cur)
        g_s[0, t * SUBLANES:(t + 1) * SUBLANES, :] = run

    ri = lax.broadcasted_iota(jnp.int32, (tc, tc), 0)
    ci = lax.broadcasted_iota(jnp.int32, (tc, tc), 1)
    causal = ri >= ci
    eye = ri == ci
    m_lane = lax.broadcasted_iota(jnp.int32, (SUBLANES, 1, LANES), 2)
    kscale = DH_B ** -0.5

    gt = _get_batches(g_s, tc, 0, LANES)
    m_old = m1_ref[...]
    m_new = m_old
    for h in range(H_B):
        cols = slice(h * DH_B, (h + 1) * DH_B)
        ig = gt[:, :, h:h + 1]
        bc = gt[:, :, H_B + h:H_B + h + 1]
        a_row = jnp.sum(jnp.where(eye, ig - bc, 0.0), axis=1, keepdims=True)
        dlog = jnp.where(causal, bc + a_row, -jnp.inf)
        inter = bc + m_old[:, :, h:h + 1]
        m = jnp.maximum(inter, jnp.max(dlog, axis=2, keepdims=True))
        p = jnp.exp(dlog - m)
        q = _get_batches(q_s, tc, h * DH_B, DH_B)
        k = _get_batches(k_s, tc, h * DH_B, DH_B) * kscale
        v_bf = _get_batches(v_s, tc, h * DH_B, DH_B).astype(BF16)
        q_bf = q.astype(BF16)
        sc = _bdot_nt(q_bf, k.astype(BF16)) * p
        w_inter = jnp.exp(inter - m)
        c0 = c_s[h]
        n0 = n1_ref[:, :, cols]
        num = _bdot(sc.astype(BF16), v_bf) + w_inter * _bdot(q_bf, c0.astype(BF16))
        den = jnp.sum(sc, axis=2, keepdims=True) + w_inter * jnp.sum(q * n0, axis=2, keepdims=True)
        hv = num * (1.0 / jnp.maximum(jnp.abs(den), jnp.exp(-m)))
        _put_batches(h_s, tc, h * DH_B, _head_norm(hv))
        m_end = m[:, tc - 1:tc, :]
        w_end = jnp.exp(bc[:, tc - 1:tc, :] - bc + ig - m_end)
        decay = jnp.exp(inter[:, tc - 1:tc, :] - m_end)
        kw = k * w_end
        c_s[h] = decay * c0 + _bdot_tn(kw.astype(BF16), v_bf)
        n1_ref[:, :, cols] = decay * n0 + jnp.sum(kw, axis=1, keepdims=True)
        m_new = jnp.where(m_lane == h, m_end, m_new)
    m1_ref[...] = m_new

    y = _get_all(h_s) * nw_ref[...] + skip_ref[...] * xc_s[...]
    gb_ref[...] = (y * _silu(zb_ref[...])).astype(BF16)

    @pl.when(c == pl.num_programs(1) - 1)
    def _():
        for h in range(H_B):
            state_copy(h, True).start()
        for h in range(H_B):
            state_copy(h, True).wait()


def mlstm_mixer(u3, buf0, c0, n0, m0, w, tc):
    nbg, rows, _ = u3.shape
    rc = tc * SUBLANES
    nch = rows // rc
    grp = lambda shape: pl.BlockSpec((None,) + shape, lambda g, c: (g, 0, 0))
    per_b = lambda width: pl.BlockSpec((SUBLANES, 1, width), lambda g, c: (g, 0, 0))
    blocked = pltpu.VMEM((W_B // LANES, rc, LANES), F32)
    n_given = c0.shape[0] // SUBLANES
    return pl.pallas_call(
        functools.partial(_mlstm_kernel, tc=tc, n_given=n_given),
        grid=(nbg, nch),
        in_specs=[pl.BlockSpec((None, rc, W_B), lambda g, c: (g, c, 1)),
                  pl.BlockSpec((None, rc, W_B), lambda g, c: (g, c, 2)),
                  grp((HIST, W_B)),
                  pl.BlockSpec(memory_space=pl.ANY),
                  per_b(W_B), per_b(LANES),
                  _const_spec((CONV_K, W_B)), _const_spec((1, W_B)),
                  _const_spec((H_B, DH_B, DH_B)), _const_spec((H_B, DH_B, DH_B)), _const_spec((H_B, DH_B, DH_B)),
                  _const_spec((3 * W_B, LANES)), _const_spec((1, LANES)),
                  _const_spec((1, W_B)), _const_spec((1, W_B))],
        out_specs=[pl.BlockSpec((None, rc, W_B), lambda g, c: (g, c, 0)),
                   pl.BlockSpec(memory_space=pl.ANY),
                   per_b(W_B), per_b(LANES), grp((HIST, W_B))],
        out_shape=[jax.ShapeDtypeStruct((nbg, rows, W_B), BF16),
                   jax.ShapeDtypeStruct((nbg * SUBLANES,) + c0.shape[1:], F32),
                   jax.ShapeDtypeStruct((nbg * SUBLANES, 1, W_B), F32),
                   jax.ShapeDtypeStruct((nbg * SUBLANES, 1, LANES), F32),
                   jax.ShapeDtypeStruct((nbg, HIST, W_B), F32)],
        scratch_shapes=[pltpu.VMEM((rc + HIST, W_B), F32),
                        pltpu.VMEM((rc, W_B), F32), blocked, blocked, blocked,
                        pltpu.VMEM((1, rc, LANES), F32), blocked,
                        pltpu.VMEM((H_B, SUBLANES, DH_B, DH_B), F32),
                        pltpu.SemaphoreType.DMA((H_B,))],
        compiler_params=_cparams(("parallel", "arbitrary")),
        name="mlstm_mixer",
    )(u3, u3, buf0, c0, n0, m0, w["ml_cw"], w["ml_cb"], w["ml_wq"], w["ml_wk"], w["ml_wv"],
      w["ml_wif"], w["ml_bif"], w["ml_nw"], w["ml_skip"])


def _ret_log_decay(h):
    return float(np.log1p(-np.exp2(-5.0 - h)))


def _retention_kernel(q_ref, k_ref, v_ref, zc_ref, cos_ref, sin_ref, s0_hbm, nw_ref,
                      gc_ref, s1_hbm, q_s, k_s, v_s, o_s, st_s, sem, *, tc, n_given):
    g = pl.program_id(0)
    c = pl.program_id(1)

    def state_copy(h, to_hbm):
        hbm = (s1_hbm if to_hbm else s0_hbm).at[pl.ds(g * SUBLANES, SUBLANES), h]
        src, dst = (st_s.at[h], hbm) if to_hbm else (hbm, st_s.at[h])
        return pltpu.make_async_copy(src, dst, sem.at[h])

    @pl.when((c == 0) & (g < n_given))
    def _():
        for h in range(H_C):
            state_copy(h, False).start()
        for h in range(H_C):
            state_copy(h, False).wait()

    @pl.when((c == 0) & (g >= n_given))
    def _():
        st_s[...] = jnp.zeros(st_s.shape, F32)

    cos, sin = cos_ref[...], sin_ref[...]
    _put_cols(v_s, 0, v_ref[...])
    for h in range(H_C):
        cols = slice(h * DK_C, (h + 1) * DK_C)
        for src, dst, scale in ((q_ref, q_s, None), (k_ref, k_s, DK_C ** -0.5)):
            x = src[:, cols]
            r = x * cos + pltpu.roll(x, DK_C // 2, 1) * sin
            dst[h] = r if scale is None else r * scale

    ri = lax.broadcasted_iota(jnp.int32, (tc, tc), 0)
    ci = lax.broadcasted_iota(jnp.int32, (tc, tc), 1)
    diff = (ri - ci).astype(F32)
    tcol = lax.broadcasted_iota(jnp.int32, (tc, 1), 0).astype(F32)
    dmask, xi, zeta, gfull = [], [], [], []
    for h in range(H_C):
        lg = _ret_log_decay(h)
        dmask.append(jnp.where(diff >= 0, jnp.exp(lg * jnp.maximum(diff, 0.0)), 0.0))
        xi.append(jnp.exp(lg * (tcol + 1.0)))
        zeta.append(jnp.exp(lg * (tc - 1.0 - tcol)))
        gfull.append(float(np.exp(np.float32(lg) * np.float32(tc))))

    for h in range(H_C):
        q_bf = _get_batches(q_s, tc, h * DK_C, DK_C).astype(BF16)
        k = _get_batches(k_s, tc, h * DK_C, DK_C)
        v_bf = _get_batches(v_s, tc, h * DV_C, DV_C).astype(BF16)
        s0 = st_s[h]
        inner = _bdot_nt(q_bf, k.astype(BF16)) * dmask[h]
        o = _bdot(inner.astype(BF16), v_bf) + xi[h] * _bdot(q_bf, s0.astype(BF16))
        _put_batches(o_s, tc, h * DV_C, _head_norm(o))
        st_s[h] = gfull[h] * s0 + _bdot_tn((k * zeta[h]).astype(BF16), v_bf)

    gc_ref[...] = (_get_all(o_s) * nw_ref[...] * _silu(zc_ref[...])).astype(BF16)

    @pl.when(c == pl.num_programs(1) - 1)
    def _():
        for h in range(H_C):
            state_copy(h, True).start()
        for h in range(H_C):
            state_copy(h, True).wait()


def retention_mixer(u3, cos, sin, s0, w, tc):
    nbg, rows, _ = u3.shape
    rc = tc * SUBLANES
    nch = rows // rc
    n_given = s0.shape[0] // SUBLANES
    return pl.pallas_call(
        functools.partial(_retention_kernel, tc=tc, n_given=n_given),
        grid=(nbg, nch),
        in_specs=[pl.BlockSpec((None, rc, QK_C), lambda g, c: (g, c, 0)),
                  pl.BlockSpec((None, rc, QK_C), lambda g, c: (g, c, 1)),
                  pl.BlockSpec((None, rc, W_C), lambda g, c: (g, c, 1)),
                  pl.BlockSpec((None, rc, W_C), lambda g, c: (g, c, 2)),
                  pl.BlockSpec((None, rc, DK_C), lambda g, c: (g, c, 0)),
                  pl.BlockSpec((None, rc, DK_C), lambda g, c: (g, c, 0)),
                  pl.BlockSpec(memory_space=pl.ANY),
                  _const_spec((1, W_C))],
        out_specs=[pl.BlockSpec((None, rc, W_C), lambda g, c: (g, c, 0)),
                   pl.BlockSpec(memory_space=pl.ANY)],
        out_shape=[jax.ShapeDtypeStruct((nbg, rows, W_C), BF16),
                   jax.ShapeDtypeStruct((nbg * SUBLANES,) + s0.shape[1:], F32)],
        scratch_shapes=[pltpu.VMEM((QK_C // LANES, rc, LANES), F32), pltpu.VMEM((QK_C // LANES, rc, LANES), F32),
                        pltpu.VMEM((W_C // LANES, rc, LANES), F32), pltpu.VMEM((W_C // LANES, rc, LANES), F32),
                        pltpu.VMEM((H_C, SUBLANES, DK_C, DV_C), F32),
                        pltpu.SemaphoreType.DMA((H_C,))],
        compiler_params=_cparams(("parallel", "arbitrary")),
        name="retention_mixer",
    )(u3, u3, u3, u3, cos, sin, s0, w["ret_nw"])


def _rglru_kernel(xd_ref, zd_ref, buf0_ref, h0_ref, cw_ref, cb_ref, wa_ref, ba_ref, wx_ref, bx_ref, lam_ref,
                  gd_ref, h1_ref, buf1_ref, xpad, a_s, b_s, *, tc):
    c = pl.program_id(1)
    rc = tc * SUBLANES

    @pl.when(c == 0)
    def _():
        h1_ref[...] = h0_ref[...]

    xc = _conv_step(c, xd_ref, buf0_ref, xpad, cw_ref, cb_ref, buf1_ref, rc)
    xc_bf = xc.astype(BF16)
    nblk = W_D // MXU_DIM
    rg, ig = [], []
    for j in range(nblk):
        lhs = xc_bf[:, j * MXU_DIM:(j + 1) * MXU_DIM]
        rg.append(_dot(lhs, wa_ref[j]))
        ig.append(_dot(lhs, wx_ref[j]))
    r = _sigmoid(jnp.concatenate(rg, axis=-1) + ba_ref[...])
    i = _sigmoid(jnp.concatenate(ig, axis=-1) + bx_ref[...])
    log_a = -LRU_C * r * _softplus(-lam_ref[...])
    a = jnp.exp(log_a)
    a_s[...] = a
    b_s[...] = jnp.sqrt(-jnp.tanh(log_a) * (a * a + 1.0)) * (i * xc)

    def step(t, h):
        r0 = pl.multiple_of(t * SUBLANES, SUBLANES)
        h = a_s[pl.ds(r0, SUBLANES), :] * h + b_s[pl.ds(r0, SUBLANES), :]
        b_s[pl.ds(r0, SUBLANES), :] = h
        return h

    h1_ref[...] = lax.fori_loop(0, tc, step, h1_ref[...], unroll=4)
    gd_ref[...] = (b_s[...] * _silu(zd_ref[...])).astype(BF16)


def rglru_mixer(u3, buf0, h0, w, tc):
    nbg, rows, _ = u3.shape
    rc = tc * SUBLANES
    nch = rows // rc
    nblk = W_D // MXU_DIM
    grp = lambda shape: pl.BlockSpec((None,) + shape, lambda g, c: (g, 0, 0))
    return pl.pallas_call(
        functools.partial(_rglru_kernel, tc=tc),
        grid=(nbg, nch),
        in_specs=[pl.BlockSpec((None, rc, W_D), lambda g, c: (g, c, 3)),
                  pl.BlockSpec((None, rc, W_D), lambda g, c: (g, c, 4)),
                  grp((HIST, W_D)), grp((SUBLANES, W_D)),
                  _const_spec((CONV_K, W_D)), _const_spec((1, W_D)),
                  _const_spec((nblk, MXU_DIM, MXU_DIM)), _const_spec((1, W_D)),
                  _const_spec((nblk, MXU_DIM, MXU_DIM)), _const_spec((1, W_D)),
                  _const_spec((1, W_D))],
        out_specs=[pl.BlockSpec((None, rc, W_D), lambda g, c: (g, c, 0)),
                   grp((SUBLANES, W_D)), grp((HIST, W_D))],
        out_shape=[jax.ShapeDtypeStruct((nbg, rows, W_D), BF16),
                   jax.ShapeDtypeStruct((nbg, SUBLANES, W_D), F32),
                   jax.ShapeDtypeStruct((nbg, HIST, W_D), F32)],
        scratch_shapes=[pltpu.VMEM((rc + HIST, W_D), F32), pltpu.VMEM((rc, W_D), F32), pltpu.VMEM((rc, W_D), F32)],
        compiler_params=_cparams(("parallel", "arbitrary")),
        name="rglru_mixer",
    )(u3, u3, buf0, h0, w["lru_cw"], w["lru_cb"], w["lru_wa"], w["lru_ba"], w["lru_wx"], w["lru_bx"], w["lru_lam"])


def _to_tm(x):
    b, t, c = x.shape
    return x.reshape(b // SUBLANES, SUBLANES, t, c).swapaxes(1, 2).reshape(b // SUBLANES, t * SUBLANES, c)


def _from_tm(x, t):
    nbg, _, c = x.shape
    return x.reshape(nbg, t, SUBLANES, c).swapaxes(1, 2).reshape(nbg * SUBLANES, t, c)


def _rope_tables(pos):
    half = DK_C // 2
    inv = ROPE_BASE ** (-jnp.arange(half, dtype=F32) / half)
    ang = pos[..., None] * inv
    cos = jnp.cos(ang)
    sin = jnp.sin(ang)
    cos = jnp.concatenate([cos, cos], axis=-1)
    sin = jnp.concatenate([-sin, sin], axis=-1)
    rep = lambda a: jnp.repeat(a, SUBLANES, axis=1)
    return rep(cos), rep(sin)


def _prep_weights(p):
    w = {}
    w["norm0"] = p["norm_w"][0][None]
    w["norm1"] = p["norm_w"][1][None]
    w["final_norm"] = p["final_norm_w"][None]
    w["ev_w_in"] = p["ev_w_in"][0].astype(BF16)
    w["ev_w_out_a"] = p["ev_w_out"][0][:W_A].astype(BF16)
    w["ev_w_out_b"] = p["ev_w_out"][0][W_A:].astype(BF16)
    are, aim, bbre, bbim = s5_prep(p["s5_lambda_re"][0], p["s5_lambda_im"][0], p["s5_log_dt"][0],
                                   p["s5_b_re"][0], p["s5_b_im"][0])
    per = MXU_DIM // S5_GROUP
    w["s5_are"] = are.reshape(1, W_S)
    w["s5_aim"] = aim.reshape(1, W_S)
    w["s5_wbre"] = _block_diag(bbre, per).astype(BF16)
    w["s5_wbim"] = _block_diag(bbim, per).astype(BF16)
    w["s5_wcre"] = _block_diag(jnp.swapaxes(p["s5_c_re"][0], 1, 2), per).astype(BF16)
    w["s5_wcim"] = _block_diag(-jnp.swapaxes(p["s5_c_im"][0], 1, 2), per).astype(BF16)
    w["s5_d"] = p["s5_d"][0][None]
    w["s5_wglu"] = p["s5_w_glu"][0].astype(BF16)
    w["s5_bglu"] = p["s5_b_glu"][0][None]
    w["ml_cw"] = p["ml_conv_w"][0]
    w["ml_cb"] = p["ml_conv_b"][0][None]
    w["ml_wq"] = p["ml_wq"][0].astype(BF16)
    w["ml_wk"] = p["ml_wk"][0].astype(BF16)
    w["ml_wv"] = p["ml_wv"][0].astype(BF16)
    w["ml_wif"] = jnp.pad(p["ml_w_if"][0], ((0, 0), (0, LANES - 2 * H_B))).astype(BF16)
    w["ml_bif"] = jnp.pad(p["ml_b_if"][0], (0, LANES - 2 * H_B))[None]
    w["ml_nw"] = p["ml_norm_w"][0][None]
    w["ml_skip"] = p["ml_skip"][0][None]
    w["od_w_in"] = p["od_w_in"][0].astype(BF16)
    w["od_w_out_c"] = p["od_w_out"][0][:W_C].astype(BF16)
    w["od_w_out_d"] = p["od_w_out"][0][W_C:].astype(BF16)
    w["ret_nw"] = p["ret_norm_w"][0][None]
    w["lru_cw"] = p["lru_conv_w"][0]
    w["lru_cb"] = p["lru_conv_b"][0][None]
    w["lru_wa"] = _block_diag(p["lru_w_a"][0], MXU_DIM // BD_D).astype(BF16)
    w["lru_ba"] = p["lru_b_a"][0][None]
    w["lru_wx"] = _block_diag(p["lru_w_x"][0], MXU_DIM // BD_D).astype(BF16)
    w["lru_bx"] = p["lru_b_x"][0][None]
    w["lru_lam"] = p["lru_lambda"][0][None]
    return w


def _conv_to_tm(buf):
    return _to_tm(buf)


def _trunk(x, pos, st, w, tc):
    new = {}
    u = proj_in(x, w["norm0"], w["ev_w_in"], True)
    ga, new["s5_re"], new["s5_im"] = s5_mixer(u, st["s5_re"], st["s5_im"], w, tc)
    gb, new["ml_c"], new["ml_n"], new["ml_m"], new["ml_conv"] = mlstm_mixer(
        u, st["ml_conv"], st["ml_c"], st["ml_n"], st["ml_m"], w, tc)
    h1 = proj_out(x, ga, gb, w["ev_w_out_a"], w["ev_w_out_b"], w["final_norm"], True, False)
    u = proj_in(h1, w["norm1"], w["od_w_in"], False)
    cos, sin = _rope_tables(pos)
    gc, new["ret"] = retention_mixer(u, cos, sin, st["ret"], w, tc)
    gd, new["lru_h"], new["lru_conv"] = rglru_mixer(u, st["lru_conv"], st["lru_h"], w, tc)
    y = proj_out(h1, gc, gd, w["od_w_out_c"], w["od_w_out_d"], w["final_norm"], False, True)
    return y, new


def _states_in(s5_re, s5_im, ml_c, ml_n, ml_m, ml_conv, ret, lru_h, lru_conv):
    b = s5_re.shape[0]
    nbg = b // SUBLANES
    return dict(
        s5_re=s5_re.reshape(nbg, SUBLANES, W_S),
        s5_im=s5_im.reshape(nbg, SUBLANES, W_S),
        ml_c=ml_c,
        ml_n=ml_n.reshape(b, 1, W_B),
        ml_m=jnp.pad(ml_m, ((0, 0), (0, LANES - H_B))).reshape(b, 1, LANES),
        ml_conv=_conv_to_tm(ml_conv),
        ret=ret,
        lru_h=lru_h.reshape(nbg, SUBLANES, W_D),
        lru_conv=_conv_to_tm(lru_conv),
    )


def _states_out(st, sel=None):
    nbg = st["s5_re"].shape[0]
    b = nbg * SUBLANES
    out = dict(
        s5_re=st["s5_re"].reshape(b, G_A, S5_STATE),
        s5_im=st["s5_im"].reshape(b, G_A, S5_STATE),
        ml_c=st["ml_c"].reshape(b, H_B, DH_B, DH_B),
        ml_n=st["ml_n"].reshape(b, H_B, DH_B),
        ml_m=st["ml_m"].reshape(b, LANES)[:, :H_B],
        ml_conv=_from_tm(st["ml_conv"], CONV_K - 1),
        ret=st["ret"].reshape(b, H_C, DK_C, DV_C),
        lru_h=st["lru_h"].reshape(b, W_D),
        lru_conv=_from_tm(st["lru_conv"], CONV_K - 1),
    )
    if sel is not None:
        out = {k: v[sel] for k, v in out.items()}
    return out


_STATE_ORDER = ("s5_re", "s5_im", "ml_c", "ml_n", "ml_m", "ml_conv", "ret", "lru_h", "lru_conv")


def kernel(x_prompt, x_sample, state_s5_re, state_s5_im, state_ml_c, state_ml_n, state_ml_m, state_ml_conv, state_ret, state_lru_h, state_lru_conv, meta, norm_w, final_norm_w, ev_w_in, ev_w_out, s5_lambda_re, s5_lambda_im, s5_log_dt, s5_b_re, s5_b_im, s5_c_re, s5_c_im, s5_d, s5_w_glu, s5_b_glu, ml_conv_w, ml_conv_b, ml_wq, ml_wk, ml_wv, ml_w_if, ml_b_if, ml_norm_w, ml_skip, od_w_in, od_w_out, ret_norm_w, lru_conv_w, lru_conv_b, lru_w_a, lru_b_a, lru_w_x, lru_b_x, lru_lambda):
    p = dict(norm_w=norm_w, final_norm_w=final_norm_w, ev_w_in=ev_w_in, ev_w_out=ev_w_out,
             s5_lambda_re=s5_lambda_re, s5_lambda_im=s5_lambda_im, s5_log_dt=s5_log_dt,
             s5_b_re=s5_b_re, s5_b_im=s5_b_im, s5_c_re=s5_c_re, s5_c_im=s5_c_im, s5_d=s5_d,
             s5_w_glu=s5_w_glu, s5_b_glu=s5_b_glu, ml_conv_w=ml_conv_w, ml_conv_b=ml_conv_b,
             ml_wq=ml_wq, ml_wk=ml_wk, ml_wv=ml_wv, ml_w_if=ml_w_if, ml_b_if=ml_b_if,
             ml_norm_w=ml_norm_w, ml_skip=ml_skip, od_w_in=od_w_in, od_w_out=od_w_out,
             ret_norm_w=ret_norm_w, lru_conv_w=lru_conv_w, lru_conv_b=lru_conv_b,
             lru_w_a=lru_w_a, lru_b_a=lru_b_a, lru_w_x=lru_w_x, lru_b_x=lru_b_x, lru_lambda=lru_lambda)
    assert ev_w_in.shape[0] == 1 and od_w_in.shape[0] == 1, "two-layer trunk"
    w = _prep_weights(p)
    bp, tp, _ = x_prompt.shape
    bs, ts, _ = x_sample.shape
    assert ts == N_META and bp % SUBLANES == 0 and bs % SUBLANES == 0 and tp % CHUNK == 0

    meta_b = jnp.broadcast_to(meta[None], (bp, N_META, D_MODEL))
    x_short = jnp.concatenate([x_sample, meta_b], axis=0)
    given = dict(s5_re=state_s5_re[0], s5_im=state_s5_im[0], ml_c=state_ml_c[0], ml_n=state_ml_n[0],
                 ml_m=state_ml_m[0], ml_conv=state_ml_conv[0], ret=state_ret[0], lru_h=state_lru_h[0],
                 lru_conv=state_lru_conv[0])
    both = {k: s if k in ("ml_c", "ret") else jnp.concatenate([s, jnp.zeros((bp,) + s.shape[1:], s.dtype)], axis=0)
            for k, s in given.items()}
    tpos = jnp.arange(N_META, dtype=F32)
    pos_short = jnp.concatenate([jnp.broadcast_to((N_META + PAST_LEN) + tpos, (bs // SUBLANES, N_META)),
                                 jnp.broadcast_to(tpos, (bp // SUBLANES, N_META))], axis=0)
    y_short, st_short = _trunk(x_short, pos_short, _states_in(**both), w, N_META)
    y_sample = y_short[:bs]
    new_s = _states_out(st_short, slice(0, bs))

    nbs = bs // SUBLANES
    nbg_short = (bs + bp) // SUBLANES
    st_meta = {}
    for name, val in st_short.items():
        per_group = val.shape[0] // nbg_short
        st_meta[name] = val[nbs * per_group:]
    pos_long = jnp.broadcast_to(N_META + jnp.arange(tp, dtype=F32), (bp // SUBLANES, tp))
    y_prompt, st_long = _trunk(x_prompt, pos_long, st_meta, w, CHUNK)
    new_p = _states_out(st_long)

    return ((y_prompt, y_sample)
            + tuple(new_p[k][None] for k in _STATE_ORDER)
            + tuple(new_s[k][None] for k in _STATE_ORDER))
```

```python
import functools
import math

import numpy as np
import jax
import jax.numpy as jnp
from jax import lax
from jax.experimental import pallas as pl
from jax.experimental.pallas import tpu as pltpu

F32 = jnp.float32
BF16 = jnp.bfloat16

SUBLANES = 8
LANES = 128
MXU_DIM = 256
VMEM_LIMIT = 60 * 1024 * 1024

D_MODEL = 1024
CHUNK = 64
N_META = 16
PAST_LEN = 2048
EPS = 1e-6
CONV_K = 4
S5_GROUP = 16
S5_STATE = 64
W_A = D_MODEL // 2
G_A = W_A // S5_GROUP
W_S = G_A * S5_STATE
W_B = D_MODEL
H_B = 4
DH_B = W_B // H_B
W_C = D_MODEL
H_C = 4
DV_C = W_C // H_C
DK_C = DV_C // 2
QK_C = H_C * DK_C
ROPE_BASE = 10000.0
W_D = D_MODEL
H_D = 8
BD_D = W_D // H_D
LRU_C = 8.0
EV_IN = 2 * W_A + 2 * W_B
OD_IN = 2 * QK_C + 2 * W_C + 2 * W_D
HIST = (CONV_K - 1) * SUBLANES

ROW_TILE = 512
S5_COLS = 1024


def _cparams(sem):
    return pltpu.CompilerParams(dimension_semantics=sem, vmem_limit_bytes=VMEM_LIMIT)


def _const_spec(shape):
    nd = len(shape)
    return pl.BlockSpec(shape, lambda *_: (0,) * nd, pipeline_mode=pl.Buffered(1))


def _dot(a, b):
    return jnp.dot(a, b, preferred_element_type=F32)


def _dot_nt(a, b):
    return lax.dot_general(a, b, (((1,), (1,)), ((), ())), preferred_element_type=F32)


def _dot_tn(a, b):
    return lax.dot_general(a, b, (((0,), (0,)), ((), ())), preferred_element_type=F32)


def _rms(x, g):
    return x * lax.rsqrt(jnp.mean(x * x, axis=-1, keepdims=True) + EPS) * g


def _softplus(x):
    return jnp.maximum(x, 0.0) + jnp.log1p(jnp.exp(-jnp.abs(x)))


def _sigmoid(x):
    return 0.5 * jnp.tanh(0.5 * x) + 0.5


def _silu(x):
    return x * _sigmoid(x)


def _put_cols(dst3, col0, val):
    for j in range(val.shape[1] // LANES):
        dst3[col0 // LANES + j] = val[:, j * LANES:(j + 1) * LANES]


def _get_all(src3):
    return jnp.concatenate([src3[j] for j in range(src3.shape[0])], axis=-1)


def _get_batch(src3, b, tc, col0, width):
    parts = [src3[col0 // LANES + j, pl.ds(b, tc, stride=SUBLANES), :] for j in range(width // LANES)]
    return parts[0] if len(parts) == 1 else jnp.concatenate(parts, axis=-1)


def _put_batch(dst3, b, tc, col0, val):
    for j in range(val.shape[1] // LANES):
        dst3[col0 // LANES + j, pl.ds(b, tc, stride=SUBLANES), :] = val[:, j * LANES:(j + 1) * LANES]


def _get_batches(src3, tc, col0, width):
    return jnp.stack([_get_batch(src3, b, tc, col0, width) for b in range(SUBLANES)], axis=0)


def _put_batches(dst3, tc, col0, val):
    for b in range(SUBLANES):
        _put_batch(dst3, b, tc, col0, val[b])


def _bdot(a, b):
    return lax.dot_general(a, b, (((2,), (1,)), ((0,), (0,))), preferred_element_type=F32)


def _bdot_nt(a, b):
    return lax.dot_general(a, b, (((2,), (2,)), ((0,), (0,))), preferred_element_type=F32)


def _bdot_tn(a, b):
    return lax.dot_general(a, b, (((1,), (1,)), ((0,), (0,))), preferred_element_type=F32)


def _head_norm(h):
    mu = jnp.mean(h, axis=-1, keepdims=True)
    d = h - mu
    var = jnp.mean(d * d, axis=-1, keepdims=True)
    return d * lax.rsqrt(var + EPS)


def _load_bm_as_tm(x_ref, xs):
    tt = x_ref.shape[1]
    for b in range(SUBLANES):
        _put_batch(xs, b, tt, 0, x_ref[b])
    return _get_all(xs)


def _act_spec(bm, tt, width):
    if bm:
        return pl.BlockSpec((SUBLANES, tt, width), lambda g, i: (g, i, 0))
    return pl.BlockSpec((None, tt * SUBLANES, width), lambda g, i: (g, i, 0))


def _act_dims(x, bm):
    return (x.shape[0] // SUBLANES, x.shape[1]) if bm else (x.shape[0], x.shape[1] // SUBLANES)


def _proj_in_kernel(x_ref, g_ref, w_ref, o_ref, *scratch, x_bm):
    x = _load_bm_as_tm(x_ref, scratch[0]) if x_bm else x_ref[...]
    o_ref[...] = _dot(_rms(x, g_ref[...]).astype(BF16), w_ref[...])


def proj_in(x, g, w, x_bm):
    d, n = w.shape
    nbg, t = _act_dims(x, x_bm)
    tt = min(ROW_TILE // SUBLANES, t)
    rows = tt * SUBLANES
    return pl.pallas_call(
        functools.partial(_proj_in_kernel, x_bm=x_bm),
        grid=(nbg, t // tt),
        in_specs=[_act_spec(x_bm, tt, d), _const_spec((1, d)), _const_spec((d, n))],
        out_specs=_act_spec(False, tt, n),
        out_shape=jax.ShapeDtypeStruct((nbg, t * SUBLANES, n), F32),
        scratch_shapes=[pltpu.VMEM((d // LANES, rows, LANES), F32)] if x_bm else [],
        compiler_params=_cparams(("parallel", "parallel")),
        name="proj_in",
    )(x, g, w)


def _s5_prep_kernel(lre_ref, lim_ref, ldt_ref, bre_ref, bim_ref, are_ref, aim_ref, bbre_ref, bbim_ref):
    lre, lim = lre_ref[...], lim_ref[...]
    dt = jnp.exp(ldt_ref[...])
    mag = jnp.exp(lre * dt)
    ab_re = mag * jnp.cos(lim * dt)
    ab_im = mag * jnp.sin(lim * dt)
    den = lre * lre + lim * lim
    nr = ab_re - 1.0
    k_re = (nr * lre + ab_im * lim) / den
    k_im = (ab_im * lre - nr * lim) / den
    bre, bim = bre_ref[...], bim_ref[...]
    are_ref[...] = ab_re
    aim_ref[...] = ab_im
    bbre_ref[...] = k_re * bre - k_im * bim
    bbim_ref[...] = k_re * bim + k_im * bre


def s5_prep(lam_re, lam_im, log_dt, b_re, b_im):
    rep = lambda a: jnp.repeat(a, S5_GROUP, axis=0)
    rows = G_A * S5_GROUP
    ldt = jnp.broadcast_to(rep(log_dt[:, None]), (rows, S5_STATE))
    tr = lambda b: jnp.swapaxes(b, 1, 2).reshape(rows, S5_STATE)
    outs = pl.pallas_call(
        _s5_prep_kernel,
        out_shape=[jax.ShapeDtypeStruct((rows, S5_STATE), F32)] * 4,
        name="s5_prep",
    )(rep(lam_re), rep(lam_im), ldt, tr(b_re), tr(b_im))
    are, aim, bbre, bbim = (o.reshape(G_A, S5_GROUP, S5_STATE) for o in outs)
    return are[:, 0], aim[:, 0], bbre, bbim


def _block_diag(blocks, per):
    n, r, c = blocks.shape
    b = blocks.reshape(n // per, per, r, c)
    eye = jnp.eye(per, dtype=blocks.dtype)
    return jnp.einsum("hgrc,gk->hgrkc", b, eye).reshape(n // per, per * r, per * c)


def _s5_kernel(ua_ref, za_ref, h0re_ref, h0im_ref, are_ref, aim_ref, wbre_ref, wbim_ref,
               wcre_ref, wcim_ref, d_ref, wglu_ref, bglu_ref,
               ga_ref, sre_ref, sim_ref, bre_s, bim_s, *, ts):
    s = pl.program_id(1)

    @pl.when(s == 0)
    def _():
        sre_ref[...] = h0re_ref[...]
        sim_ref[...] = h0im_ref[...]

    ua = ua_ref[...]
    ua_bf = ua.astype(BF16)
    nhalf = W_A // MXU_DIM
    hw = W_S // nhalf
    for hf in range(nhalf):
        lhs = ua_bf[:, hf * MXU_DIM:(hf + 1) * MXU_DIM]
        bre_s[:, hf * hw:(hf + 1) * hw] = _dot(lhs, wbre_ref[hf])
        bim_s[:, hf * hw:(hf + 1) * hw] = _dot(lhs, wbim_ref[hf])

    for cb in range(W_S // S5_COLS):
        cols = slice(cb * S5_COLS, (cb + 1) * S5_COLS)
        ar = jnp.broadcast_to(are_ref[:, cols], (SUBLANES, S5_COLS))
        ai = jnp.broadcast_to(aim_ref[:, cols], (SUBLANES, S5_COLS))

        def step(t, carry, cols=cols, ar=ar, ai=ai):
            hr, hi = carry
            r0 = pl.multiple_of(t * SUBLANES, SUBLANES)
            nr = ar * hr - ai * hi + bre_s[pl.ds(r0, SUBLANES), cols]
            ni = ar * hi + ai * hr + bim_s[pl.ds(r0, SUBLANES), cols]
            bre_s[pl.ds(r0, SUBLANES), cols] = nr
            bim_s[pl.ds(r0, SUBLANES), cols] = ni
            return nr, ni

        hr, hi = lax.fori_loop(0, ts, step, (sre_ref[:, cols], sim_ref[:, cols]), unroll=4)
        sre_ref[:, cols] = hr
        sim_ref[:, cols] = hi

    ys = []
    for hf in range(nhalf):
        hre = bre_s[:, hf * hw:(hf + 1) * hw].astype(BF16)
        him = bim_s[:, hf * hw:(hf + 1) * hw].astype(BF16)
        ys.append(_dot(hre, wcre_ref[hf]) + _dot(him, wcim_ref[hf]))
    y = jnp.concatenate(ys, axis=-1) + d_ref[...] * ua
    y = jax.nn.gelu(y, approximate=True)
    y = y * _sigmoid(_dot(y.astype(BF16), wglu_ref[...]) + bglu_ref[...])
    ga_ref[...] = (y * _silu(za_ref[...])).astype(BF16)


def s5_mixer(u3, h0re, h0im, w, ts):
    nbg, rows, _ = u3.shape
    rs = ts * SUBLANES
    nsteps = rows // rs
    st_spec = pl.BlockSpec((None, SUBLANES, W_S), lambda g, s: (g, 0, 0))
    nhalf = W_A // MXU_DIM
    hw = W_S // nhalf
    return pl.pallas_call(
        functools.partial(_s5_kernel, ts=ts),
        grid=(nbg, nsteps),
        in_specs=[pl.BlockSpec((None, rs, W_A), lambda g, s: (g, s, 0)),
                  pl.BlockSpec((None, rs, W_A), lambda g, s: (g, s, 1)),
                  st_spec, st_spec,
                  _const_spec((1, W_S)), _const_spec((1, W_S)),
                  _const_spec((nhalf, MXU_DIM, hw)), _const_spec((nhalf, MXU_DIM, hw)),
                  _const_spec((nhalf, hw, MXU_DIM)), _const_spec((nhalf, hw, MXU_DIM)),
                  _const_spec((1, W_A)), _const_spec((W_A, W_A)), _const_spec((1, W_A))],
        out_specs=[pl.BlockSpec((None, rs, W_A), lambda g, s: (g, s, 0)), st_spec, st_spec],
        out_shape=[jax.ShapeDtypeStruct((nbg, rows, W_A), BF16),
                   jax.ShapeDtypeStruct((nbg, SUBLANES, W_S), F32),
                   jax.ShapeDtypeStruct((nbg, SUBLANES, W_S), F32)],
        scratch_shapes=[pltpu.VMEM((rs, W_S), F32), pltpu.VMEM((rs, W_S), F32)],
        compiler_params=_cparams(("parallel", "arbitrary")),
        name="s5_mixer",
    )(u3, u3, h0re, h0im, w["s5_are"], w["s5_aim"], w["s5_wbre"], w["s5_wbim"],
      w["s5_wcre"], w["s5_wcim"], w["s5_d"], w["s5_wglu"], w["s5_bglu"])


def _conv_history(c, buf0_ref, xpad, buf1_ref, rc):
    @pl.when(c == 0)
    def _():
        xpad[0:HIST, :] = buf0_ref[...]
        buf1_ref[...] = buf0_ref[...]

    @pl.when(c > 0)
    def _():
        xpad[0:HIST, :] = xpad[rc:rc + HIST, :]


def _conv_step(live, x_ref, xpad, cw_ref, cb_ref, buf1_ref, rc):
    xpad[HIST:HIST + rc, :] = x_ref[...]
    buf1_ref[...] = jnp.where(live, xpad[rc:rc + HIST, :], buf1_ref[...])
    out = cb_ref[...]
    for tap in range(CONV_K):
        out = out + xpad[tap * SUBLANES:tap * SUBLANES + rc, :] * cw_ref[tap:tap + 1, :]
    return out


def _mlstm_kernel(x_ref, ga_ref, xb_ref, zb_ref, buf0_ref, c0_hbm, n0_ref, m0_ref, cw_ref, cb_ref,
                  wq_ref, wk_ref, wv_ref, wif_ref, bif_ref, nw_ref, skip_ref, wa_ref, wb_ref,
                  h1_ref, c1_hbm, n1_ref, m1_ref, buf1_ref,
                  xpad, xc_s, q_s, k_s, v_s, g_s, h_s, c_s, gb_s, xs, sem, *, tc, n_given):
    g = pl.program_id(0)
    c = pl.program_id(1)
    nch = pl.num_programs(1) - 1
    live = c < nch
    rc = tc * SUBLANES

    def state_copy(h, to_hbm):
        hbm = (c1_hbm if to_hbm else c0_hbm).at[pl.ds(g * SUBLANES, SUBLANES), h]
        src, dst = (c_s.at[h], hbm) if to_hbm else (hbm, c_s.at[h])
        return pltpu.make_async_copy(src, dst, sem.at[h])

    @pl.when(c == 0)
    def _():
        n1_ref[...] = n0_ref[...]
        m1_ref[...] = m0_ref[...]
        gb_s[...] = jnp.zeros(gb_s.shape, BF16)

    @pl.when(c == nch)
    def _():
        for h in range(H_B):
            state_copy(h, True).start()
        for h in range(H_B):
            state_copy(h, True).wait()

    @pl.when((c == 0) & (g < n_given))
    def _():
        for h in range(H_B):
            state_copy(h, False).start()
        for h in range(H_B):
            state_copy(h, False).wait()

    @pl.when((c == 0) & (g >= n_given))
    def _():
        c_s[...] = jnp.zeros(c_s.shape, F32)

    _conv_history(c, buf0_ref, xpad, buf1_ref, rc)

    for b in range(SUBLANES):
        _put_batch(xs, b, tc, 0, x_ref[b])

    xc = _silu(_conv_step(live, xb_ref, xpad, cw_ref, cb_ref, buf1_ref, rc))
    xc_s[...] = xc
    xc_bf = xc.astype(BF16)
    xb_bf = xb_ref[...].astype(BF16)
    gates = bif_ref[...]
    for h in range(H_B):
        cols = slice(h * DH_B, (h + 1) * DH_B)
        for i, (src, w_ref, dst) in enumerate(((xc_bf, wq_ref, q_s), (xc_bf, wk_ref, k_s), (xb_bf, wv_ref, v_s))):
            r = _dot(src[:, cols], w_ref[h])
            _put_cols(dst, h * DH_B, r)
            gates = gates + _dot(r.astype(BF16), wif_ref[i * W_B + h * DH_B:i * W_B + (h + 1) * DH_B, :])
    lane = lax.broadcasted_iota(jnp.int32, (rc, LANES), 1)
    gates = jnp.where(lane < H_B, gates, -_softplus(-gates))
    is_f = lax.broadcasted_iota(jnp.int32, (SUBLANES, LANES), 1) >= H_B
    run = jnp.zeros((SUBLANES, LANES), F32)
    for t in range(tc):
        cur = gates[t * SUBLANES:(t + 1) * SUBLANES, :]
        run = jnp.where(is_f, run + cur, cur)
        g_s[0, t * SUBLANES:(t + 1) * SUBLANES, :] = run

    ri = lax.broadcasted_iota(jnp.int32, (tc, tc), 0)
    ci = lax.broadcasted_iota(jnp.int32, (tc, tc), 1)
    causal = ri >= ci
    eye = ri == ci
    m_lane = lax.broadcasted_iota(jnp.int32, (SUBLANES, 1, LANES), 2)
    kscale = DH_B ** -0.5

    gt = _get_batches(g_s, tc, 0, LANES)
    m_old = m1_ref[...]
    m_new = m_old
    for h in range(H_B):
        cols = slice(h * DH_B, (h + 1) * DH_B)
        y_a = _dot(ga_ref[...], wa_ref[:, cols])

        ig = gt[:, :, h:h + 1]
        bc = gt[:, :, H_B + h:H_B + h + 1]
        a_row = jnp.sum(jnp.where(eye, ig - bc, 0.0), axis=1, keepdims=True)
        dlog = jnp.where(causal, bc + a_row, -jnp.inf)
        inter = bc + m_old[:, :, h:h + 1]
        m = jnp.maximum(inter, jnp.max(dlog, axis=2, keepdims=True))
        p = jnp.exp(dlog - m)
        q = _get_batches(q_s, tc, h * DH_B, DH_B)
        k = _get_batches(k_s, tc, h * DH_B, DH_B) * kscale
        v_bf = _get_batches(v_s, tc, h * DH_B, DH_B).astype(BF16)
        q_bf = q.astype(BF16)
        sc = _bdot_nt(q_bf, k.astype(BF16)) * p
        w_inter = jnp.exp(inter - m)
        c0 = c_s[h]
        n0 = n1_ref[:, :, cols]
        num = _bdot(sc.astype(BF16), v_bf) + w_inter * _bdot(q_bf, c0.astype(BF16))
        den = jnp.sum(sc, axis=2, keepdims=True) + w_inter * jnp.sum(q * n0, axis=2, keepdims=True)
        hv = num * (1.0 / jnp.maximum(jnp.abs(den), jnp.exp(-m)))
        y_b = _dot(gb_s[...], wb_ref[:, cols])
        _put_batches(h_s, tc, h * DH_B, _head_norm(hv))
        m_end = m[:, tc - 1:tc, :]
        w_end = jnp.exp(bc[:, tc - 1:tc, :] - bc + ig - m_end)
        decay = jnp.exp(inter[:, tc - 1:tc, :] - m_end)
        kw = k * w_end
        c_s[h] = decay * c0 + _bdot_tn(kw.astype(BF16), v_bf)
        n1_ref[:, :, cols] = jnp.where(live, decay * n0 + jnp.sum(kw, axis=1, keepdims=True), n0)
        m_new = jnp.where(m_lane == h, m_end, m_new)
        res = jnp.concatenate([xs[j] for j in range(h * DH_B // LANES, (h + 1) * DH_B // LANES)], axis=-1)
        h1_ref[:, cols] = res + y_a + y_b
    m1_ref[...] = jnp.where(live, m_new, m_old)

    y = _get_all(h_s) * nw_ref[...] + skip_ref[...] * xc_s[...]
    gb_s[...] = (y * _silu(zb_ref[...])).astype(BF16)


def mlstm_mixer(x, ga, u3, buf0, c0, n0, m0, w, tc):
    nbg, rows, _ = u3.shape
    rc = tc * SUBLANES
    nch = rows // rc
    grp = lambda shape: pl.BlockSpec((None,) + shape, lambda g, c: (g, 0, 0))
    per_b = lambda width: pl.BlockSpec((SUBLANES, 1, width), lambda g, c: (g, 0, 0))
    cur = lambda blk: (lambda g, c: (g, jnp.minimum(c, nch - 1), blk))
    prev = lambda g, c: (g, jnp.maximum(c - 1, 0), 0)
    blocked = pltpu.VMEM((W_B // LANES, rc, LANES), F32)
    n_given = c0.shape[0] // SUBLANES
    return pl.pallas_call(
        functools.partial(_mlstm_kernel, tc=tc, n_given=n_given),
        grid=(nbg, nch + 1),
        in_specs=[pl.BlockSpec((SUBLANES, tc, D_MODEL), prev),
                  pl.BlockSpec((None, rc, W_A), prev),
                  pl.BlockSpec((None, rc, W_B), cur(1)),
                  pl.BlockSpec((None, rc, W_B), cur(2)),
                  grp((HIST, W_B)),
                  pl.BlockSpec(memory_space=pl.ANY),
                  per_b(W_B), per_b(LANES),
                  _const_spec((CONV_K, W_B)), _const_spec((1, W_B)),
                  _const_spec((H_B, DH_B, DH_B)), _const_spec((H_B, DH_B, DH_B)), _const_spec((H_B, DH_B, DH_B)),
                  _const_spec((3 * W_B, LANES)), _const_spec((1, LANES)),
                  _const_spec((1, W_B)), _const_spec((1, W_B)),
                  _const_spec((W_A, D_MODEL)), _const_spec((W_B, D_MODEL))],
        out_specs=[pl.BlockSpec((None, rc, D_MODEL), prev),
                   pl.BlockSpec(memory_space=pl.ANY),
                   per_b(W_B), per_b(LANES), grp((HIST, W_B))],
        out_shape=[jax.ShapeDtypeStruct((nbg, rows, D_MODEL), F32),
                   jax.ShapeDtypeStruct((nbg * SUBLANES,) + c0.shape[1:], F32),
                   jax.ShapeDtypeStruct((nbg * SUBLANES, 1, W_B), F32),
                   jax.ShapeDtypeStruct((nbg * SUBLANES, 1, LANES), F32),
                   jax.ShapeDtypeStruct((nbg, HIST, W_B), F32)],
        scratch_shapes=[pltpu.VMEM((rc + HIST, W_B), F32),
                        pltpu.VMEM((rc, W_B), F32), blocked, blocked, blocked,
                        pltpu.VMEM((1, rc, LANES), F32), blocked,
                        pltpu.VMEM((H_B, SUBLANES, DH_B, DH_B), F32),
                        pltpu.VMEM((rc, W_B), BF16), blocked,
                        pltpu.SemaphoreType.DMA((H_B,))],
        compiler_params=_cparams(("parallel", "arbitrary")),
        name="mlstm_mixer",
    )(x, ga, u3, u3, buf0, c0, n0, m0, w["ml_cw"], w["ml_cb"], w["ml_wq"], w["ml_wk"], w["ml_wv"],
      w["ml_wif"], w["ml_bif"], w["ml_nw"], w["ml_skip"], w["ev_w_out_a"], w["ev_w_out_b"])


def _ret_log_decay(h):
    return float(np.log1p(-np.exp2(-5.0 - h)))


def _retention_kernel(q_ref, k_ref, v_ref, zc_ref, cos_ref, sin_ref, s0_hbm, nw_ref,
                      gc_ref, s1_hbm, q_s, k_s, v_s, o_s, st_s, sem, *, tc, n_given):
    g = pl.program_id(0)
    c = pl.program_id(1)

    def state_copy(h, to_hbm):
        hbm = (s1_hbm if to_hbm else s0_hbm).at[pl.ds(g * SUBLANES, SUBLANES), h]
        src, dst = (st_s.at[h], hbm) if to_hbm else (hbm, st_s.at[h])
        return pltpu.make_async_copy(src, dst, sem.at[h])

    @pl.when((c == 0) & (g < n_given))
    def _():
        for h in range(H_C):
            state_copy(h, False).start()
        for h in range(H_C):
            state_copy(h, False).wait()

    @pl.when((c == 0) & (g >= n_given))
    def _():
        st_s[...] = jnp.zeros(st_s.shape, F32)

    cos, sin = cos_ref[...], sin_ref[...]
    _put_cols(v_s, 0, v_ref[...])
    for h in range(H_C):
        cols = slice(h * DK_C, (h + 1) * DK_C)
        for src, dst, scale in ((q_ref, q_s, None), (k_ref, k_s, DK_C ** -0.5)):
            x = src[:, cols]
            r = x * cos + pltpu.roll(x, DK_C // 2, 1) * sin
            dst[h] = r if scale is None else r * scale

    ri = lax.broadcasted_iota(jnp.int32, (tc, tc), 0)
    ci = lax.broadcasted_iota(jnp.int32, (tc, tc), 1)
    diff = (ri - ci).astype(F32)
    tcol = lax.broadcasted_iota(jnp.int32, (tc, 1), 0).astype(F32)
    dmask, xi, zeta, gfull = [], [], [], []
    for h in range(H_C):
        lg = _ret_log_decay(h)
        dmask.append(jnp.where(diff >= 0, jnp.exp(lg * jnp.maximum(diff, 0.0)), 0.0))
        xi.append(jnp.exp(lg * (tcol + 1.0)))
        zeta.append(jnp.exp(lg * (tc - 1.0 - tcol)))
        gfull.append(float(np.exp(np.float32(lg) * np.float32(tc))))

    for h in range(H_C):
        q_bf = _get_batches(q_s, tc, h * DK_C, DK_C).astype(BF16)
        k = _get_batches(k_s, tc, h * DK_C, DK_C)
        v_bf = _get_batches(v_s, tc, h * DV_C, DV_C).astype(BF16)
        s0 = st_s[h]
        inner = _bdot_nt(q_bf, k.astype(BF16)) * dmask[h]
        o = _bdot(inner.astype(BF16), v_bf) + xi[h] * _bdot(q_bf, s0.astype(BF16))
        _put_batches(o_s, tc, h * DV_C, _head_norm(o))
        st_s[h] = gfull[h] * s0 + _bdot_tn((k * zeta[h]).astype(BF16), v_bf)

    gc_ref[...] = (_get_all(o_s) * nw_ref[...] * _silu(zc_ref[...])).astype(BF16)

    @pl.when(c == pl.num_programs(1) - 1)
    def _():
        for h in range(H_C):
            state_copy(h, True).start()
        for h in range(H_C):
            state_copy(h, True).wait()


def retention_mixer(u3, cos, sin, s0, w, tc):
    nbg, rows, _ = u3.shape
    rc = tc * SUBLANES
    nch = rows // rc
    n_given = s0.shape[0] // SUBLANES
    return pl.pallas_call(
        functools.partial(_retention_kernel, tc=tc, n_given=n_given),
        grid=(nbg, nch),
        in_specs=[pl.BlockSpec((None, rc, QK_C), lambda g, c: (g, c, 0)),
                  pl.BlockSpec((None, rc, QK_C), lambda g, c: (g, c, 1)),
                  pl.BlockSpec((None, rc, W_C), lambda g, c: (g, c, 1)),
                  pl.BlockSpec((None, rc, W_C), lambda g, c: (g, c, 2)),
                  pl.BlockSpec((None, rc, DK_C), lambda g, c: (g, c, 0)),
                  pl.BlockSpec((None, rc, DK_C), lambda g, c: (g, c, 0)),
                  pl.BlockSpec(memory_space=pl.ANY),
                  _const_spec((1, W_C))],
        out_specs=[pl.BlockSpec((None, rc, W_C), lambda g, c: (g, c, 0)),
                   pl.BlockSpec(memory_space=pl.ANY)],
        out_shape=[jax.ShapeDtypeStruct((nbg, rows, W_C), BF16),
                   jax.ShapeDtypeStruct((nbg * SUBLANES,) + s0.shape[1:], F32)],
        scratch_shapes=[pltpu.VMEM((QK_C // LANES, rc, LANES), F32), pltpu.VMEM((QK_C // LANES, rc, LANES), F32),
                        pltpu.VMEM((W_C // LANES, rc, LANES), F32), pltpu.VMEM((W_C // LANES, rc, LANES), F32),
                        pltpu.VMEM((H_C, SUBLANES, DK_C, DV_C), F32),
                        pltpu.SemaphoreType.DMA((H_C,))],
        compiler_params=_cparams(("parallel", "arbitrary")),
        name="retention_mixer",
    )(u3, u3, u3, u3, cos, sin, s0, w["ret_nw"])


def _rglru_kernel(res_ref, gc_ref, xd_ref, zd_ref, buf0_ref, h0_ref, cw_ref, cb_ref, wa_ref, ba_ref, wx_ref,
                  bx_ref, lam_ref, woc_ref, wod_ref, fn_ref,
                  y_ref, h1_ref, buf1_ref, xpad, a_s, b_s, gd_s, ys, inv_s, *, tc):
    c = pl.program_id(1)
    nch = pl.num_programs(1) - 1
    live = c < nch
    rc = tc * SUBLANES

    @pl.when(c == 0)
    def _():
        h1_ref[...] = h0_ref[...]
        gd_s[...] = jnp.zeros(gd_s.shape, BF16)

    _conv_history(c, buf0_ref, xpad, buf1_ref, rc)

    xpad[HIST:HIST + rc, :] = xd_ref[...]
    buf1_ref[...] = jnp.where(live, xpad[rc:rc + HIST, :], buf1_ref[...])
    sp = _softplus(-lam_ref[...])
    ssq = jnp.zeros((rc, 1), F32)
    for j in range(W_D // MXU_DIM):
        cols = slice(j * MXU_DIM, (j + 1) * MXU_DIM)
        y_c = _dot(gc_ref[...], woc_ref[:, cols])
        xc = cb_ref[:, cols]
        for tap in range(CONV_K):
            xc = xc + xpad[tap * SUBLANES:tap * SUBLANES + rc, cols] * cw_ref[tap:tap + 1, cols]
        lhs = xc.astype(BF16)
        r = _sigmoid(_dot(lhs, wa_ref[j]) + ba_ref[:, cols])
        y_d = _dot(gd_s[...], wod_ref[:, cols])
        i = _sigmoid(_dot(lhs, wx_ref[j]) + bx_ref[:, cols])
        log_a = -LRU_C * r * sp[:, cols]
        a = jnp.exp(log_a)
        a_s[:, cols] = a
        b_s[:, cols] = jnp.sqrt(-jnp.tanh(log_a) * (a * a + 1.0)) * (i * xc)
        yj = res_ref[:, cols] + y_c + y_d
        _put_cols(ys, j * MXU_DIM, yj)
        ssq = ssq + jnp.sum(yj * yj, axis=-1, keepdims=True)

    inv_s[0] = jnp.broadcast_to(lax.rsqrt(ssq * (1.0 / D_MODEL) + EPS), (rc, LANES))
    for b in range(SUBLANES):
        inv_b = _get_batch(inv_s, b, tc, 0, LANES)[:, 0:1]
        y_ref[b] = _get_batch(ys, b, tc, 0, D_MODEL) * inv_b * fn_ref[...]

    def step(t, h):
        r0 = pl.multiple_of(t * SUBLANES, SUBLANES)
        h = a_s[pl.ds(r0, SUBLANES), :] * h + b_s[pl.ds(r0, SUBLANES), :]
        b_s[pl.ds(r0, SUBLANES), :] = h
        return h

    h_old = h1_ref[...]
    h1_ref[...] = jnp.where(live, lax.fori_loop(0, tc, step, h_old, unroll=4), h_old)
    gd_s[...] = (b_s[...] * _silu(zd_ref[...])).astype(BF16)


def rglru_mixer(res, gc, u3, buf0, h0, w, tc):
    nbg, rows, _ = u3.shape
    rc = tc * SUBLANES
    nch = rows // rc
    nblk = W_D // MXU_DIM
    grp = lambda shape: pl.BlockSpec((None,) + shape, lambda g, c: (g, 0, 0))
    cur = lambda blk: (lambda g, c: (g, jnp.minimum(c, nch - 1), blk))
    prev = lambda g, c: (g, jnp.maximum(c - 1, 0), 0)
    return pl.pallas_call(
        functools.partial(_rglru_kernel, tc=tc),
        grid=(nbg, nch + 1),
        in_specs=[pl.BlockSpec((None, rc, D_MODEL), prev),
                  pl.BlockSpec((None, rc, W_C), prev),
                  pl.BlockSpec((None, rc, W_D), cur(3)),
                  pl.BlockSpec((None, rc, W_D), cur(4)),
                  grp((HIST, W_D)), grp((SUBLANES, W_D)),
                  _const_spec((CONV_K, W_D)), _const_spec((1, W_D)),
                  _const_spec((nblk, MXU_DIM, MXU_DIM)), _const_spec((1, W_D)),
                  _const_spec((nblk, MXU_DIM, MXU_DIM)), _const_spec((1, W_D)),
                  _const_spec((1, W_D)),
                  _const_spec((W_C, D_MODEL)), _const_spec((W_D, D_MODEL)), _const_spec((1, D_MODEL))],
        out_specs=[pl.BlockSpec((SUBLANES, tc, D_MODEL), prev),
                   grp((SUBLANES, W_D)), grp((HIST, W_D))],
        out_shape=[jax.ShapeDtypeStruct((nbg * SUBLANES, rows // SUBLANES, D_MODEL), F32),
                   jax.ShapeDtypeStruct((nbg, SUBLANES, W_D), F32),
                   jax.ShapeDtypeStruct((nbg, HIST, W_D), F32)],
        scratch_shapes=[pltpu.VMEM((rc + HIST, W_D), F32), pltpu.VMEM((rc, W_D), F32), pltpu.VMEM((rc, W_D), F32),
                        pltpu.VMEM((rc, W_D), BF16), pltpu.VMEM((D_MODEL // LANES, rc, LANES), F32),
                        pltpu.VMEM((1, rc, LANES), F32)],
        compiler_params=_cparams(("parallel", "arbitrary")),
        name="rglru_mixer",
    )(res, gc, u3, u3, buf0, h0, w["lru_cw"], w["lru_cb"], w["lru_wa"], w["lru_ba"], w["lru_wx"], w["lru_bx"],
      w["lru_lam"], w["od_w_out_c"], w["od_w_out_d"], w["final_norm"])


def _to_tm(x):
    b, t, c = x.shape
    return x.reshape(b // SUBLANES, SUBLANES, t, c).swapaxes(1, 2).reshape(b // SUBLANES, t * SUBLANES, c)


def _from_tm(x, t):
    nbg, _, c = x.shape
    return x.reshape(nbg, t, SUBLANES, c).swapaxes(1, 2).reshape(nbg * SUBLANES, t, c)


def _rope_tables(pos):
    half = DK_C // 2
    inv = ROPE_BASE ** (-jnp.arange(half, dtype=F32) / half)
    ang = pos[..., None] * inv
    cos = jnp.cos(ang)
    sin = jnp.sin(ang)
    cos = jnp.concatenate([cos, cos], axis=-1)
    sin = jnp.concatenate([-sin, sin], axis=-1)
    rep = lambda a: jnp.repeat(a, SUBLANES, axis=1)
    return rep(cos), rep(sin)


def _prep_weights(p):
    w = {}
    w["norm0"] = p["norm_w"][0][None]
    w["norm1"] = p["norm_w"][1][None]
    w["final_norm"] = p["final_norm_w"][None]
    w["ev_w_in"] = p["ev_w_in"][0].astype(BF16)
    w["ev_w_out_a"] = p["ev_w_out"][0][:W_A].astype(BF16)
    w["ev_w_out_b"] = p["ev_w_out"][0][W_A:].astype(BF16)
    are, aim, bbre, bbim = s5_prep(p["s5_lambda_re"][0], p["s5_lambda_im"][0], p["s5_log_dt"][0],
                                   p["s5_b_re"][0], p["s5_b_im"][0])
    per = MXU_DIM // S5_GROUP
    w["s5_are"] = are.reshape(1, W_S)
    w["s5_aim"] = aim.reshape(1, W_S)
    w["s5_wbre"] = _block_diag(bbre, per).astype(BF16)
    w["s5_wbim"] = _block_diag(bbim, per).astype(BF16)
    w["s5_wcre"] = _block_diag(jnp.swapaxes(p["s5_c_re"][0], 1, 2), per).astype(BF16)
    w["s5_wcim"] = _block_diag(-jnp.swapaxes(p["s5_c_im"][0], 1, 2), per).astype(BF16)
    w["s5_d"] = p["s5_d"][0][None]
    w["s5_wglu"] = p["s5_w_glu"][0].astype(BF16)
    w["s5_bglu"] = p["s5_b_glu"][0][None]
    w["ml_cw"] = p["ml_conv_w"][0]
    w["ml_cb"] = p["ml_conv_b"][0][None]
    w["ml_wq"] = p["ml_wq"][0].astype(BF16)
    w["ml_wk"] = p["ml_wk"][0].astype(BF16)
    w["ml_wv"] = p["ml_wv"][0].astype(BF16)
    w["ml_wif"] = jnp.pad(p["ml_w_if"][0], ((0, 0), (0, LANES - 2 * H_B))).astype(BF16)
    w["ml_bif"] = jnp.pad(p["ml_b_if"][0], (0, LANES - 2 * H_B))[None]
    w["ml_nw"] = p["ml_norm_w"][0][None]
    w["ml_skip"] = p["ml_skip"][0][None]
    w["od_w_in"] = p["od_w_in"][0].astype(BF16)
    w["od_w_out_c"] = p["od_w_out"][0][:W_C].astype(BF16)
    w["od_w_out_d"] = p["od_w_out"][0][W_C:].astype(BF16)
    w["ret_nw"] = p["ret_norm_w"][0][None]
    w["lru_cw"] = p["lru_conv_w"][0]
    w["lru_cb"] = p["lru_conv_b"][0][None]
    w["lru_wa"] = _block_diag(p["lru_w_a"][0], MXU_DIM // BD_D).astype(BF16)
    w["lru_ba"] = p["lru_b_a"][0][None]
    w["lru_wx"] = _block_diag(p["lru_w_x"][0], MXU_DIM // BD_D).astype(BF16)
    w["lru_bx"] = p["lru_b_x"][0][None]
    w["lru_lam"] = p["lru_lambda"][0][None]
    return w


def _conv_to_tm(buf):
    return _to_tm(buf)


def _trunk(x, pos, st, w, tc):
    new = {}
    u = proj_in(x, w["norm0"], w["ev_w_in"], True)
    ga, new["s5_re"], new["s5_im"] = s5_mixer(u, st["s5_re"], st["s5_im"], w, tc)
    h1, new["ml_c"], new["ml_n"], new["ml_m"], new["ml_conv"] = mlstm_mixer(
        x, ga, u, st["ml_conv"], st["ml_c"], st["ml_n"], st["ml_m"], w, tc)
    u = proj_in(h1, w["norm1"], w["od_w_in"], False)
    cos, sin = _rope_tables(pos)
    gc, new["ret"] = retention_mixer(u, cos, sin, st["ret"], w, tc)
    y, new["lru_h"], new["lru_conv"] = rglru_mixer(h1, gc, u, st["lru_conv"], st["lru_h"], w, tc)
    return y, new


def _states_in(s5_re, s5_im, ml_c, ml_n, ml_m, ml_conv, ret, lru_h, lru_conv):
    b = s5_re.shape[0]
    nbg = b // SUBLANES
    return dict(
        s5_re=s5_re.reshape(nbg, SUBLANES, W_S),
        s5_im=s5_im.reshape(nbg, SUBLANES, W_S),
        ml_c=ml_c,
        ml_n=ml_n.reshape(b, 1, W_B),
        ml_m=jnp.pad(ml_m, ((0, 0), (0, LANES - H_B))).reshape(b, 1, LANES),
        ml_conv=_conv_to_tm(ml_conv),
        ret=ret,
        lru_h=lru_h.reshape(nbg, SUBLANES, W_D),
        lru_conv=_conv_to_tm(lru_conv),
    )


def _states_out(st, sel=None):
    nbg = st["s5_re"].shape[0]
    b = nbg * SUBLANES
    out = dict(
        s5_re=st["s5_re"].reshape(b, G_A, S5_STATE),
        s5_im=st["s5_im"].reshape(b, G_A, S5_STATE),
        ml_c=st["ml_c"].reshape(b, H_B, DH_B, DH_B),
        ml_n=st["ml_n"].reshape(b, H_B, DH_B),
        ml_m=st["ml_m"].reshape(b, LANES)[:, :H_B],
        ml_conv=_from_tm(st["ml_conv"], CONV_K - 1),
        ret=st["ret"].reshape(b, H_C, DK_C, DV_C),
        lru_h=st["lru_h"].reshape(b, W_D),
        lru_conv=_from_tm(st["lru_conv"], CONV_K - 1),
    )
    if sel is not None:
        out = {k: v[sel] for k, v in out.items()}
    return out


_STATE_ORDER = ("s5_re", "s5_im", "ml_c", "ml_n", "ml_m", "ml_conv", "ret", "lru_h", "lru_conv")


def kernel(x_prompt, x_sample, state_s5_re, state_s5_im, state_ml_c, state_ml_n, state_ml_m, state_ml_conv, state_ret, state_lru_h, state_lru_conv, meta, norm_w, final_norm_w, ev_w_in, ev_w_out, s5_lambda_re, s5_lambda_im, s5_log_dt, s5_b_re, s5_b_im, s5_c_re, s5_c_im, s5_d, s5_w_glu, s5_b_glu, ml_conv_w, ml_conv_b, ml_wq, ml_wk, ml_wv, ml_w_if, ml_b_if, ml_norm_w, ml_skip, od_w_in, od_w_out, ret_norm_w, lru_conv_w, lru_conv_b, lru_w_a, lru_b_a, lru_w_x, lru_b_x, lru_lambda):
    p = dict(norm_w=norm_w, final_norm_w=final_norm_w, ev_w_in=ev_w_in, ev_w_out=ev_w_out,
             s5_lambda_re=s5_lambda_re, s5_lambda_im=s5_lambda_im, s5_log_dt=s5_log_dt,
             s5_b_re=s5_b_re, s5_b_im=s5_b_im, s5_c_re=s5_c_re, s5_c_im=s5_c_im, s5_d=s5_d,
             s5_w_glu=s5_w_glu, s5_b_glu=s5_b_glu, ml_conv_w=ml_conv_w, ml_conv_b=ml_conv_b,
             ml_wq=ml_wq, ml_wk=ml_wk, ml_wv=ml_wv, ml_w_if=ml_w_if, ml_b_if=ml_b_if,
             ml_norm_w=ml_norm_w, ml_skip=ml_skip, od_w_in=od_w_in, od_w_out=od_w_out,
             ret_norm_w=ret_norm_w, lru_conv_w=lru_conv_w, lru_conv_b=lru_conv_b,
             lru_w_a=lru_w_a, lru_b_a=lru_b_a, lru_w_x=lru_w_x, lru_b_x=lru_b_x, lru_lambda=lru_lambda)
    assert ev_w_in.shape[0] == 1 and od_w_in.shape[0] == 1, "two-layer trunk"
    w = _prep_weights(p)
    bp, tp, _ = x_prompt.shape
    bs, ts, _ = x_sample.shape
    assert ts == N_META and bp % SUBLANES == 0 and bs % SUBLANES == 0 and tp % CHUNK == 0

    meta_b = jnp.broadcast_to(meta[None], (bp, N_META, D_MODEL))
    x_short = jnp.concatenate([x_sample, meta_b], axis=0)
    given = dict(s5_re=state_s5_re[0], s5_im=state_s5_im[0], ml_c=state_ml_c[0], ml_n=state_ml_n[0],
                 ml_m=state_ml_m[0], ml_conv=state_ml_conv[0], ret=state_ret[0], lru_h=state_lru_h[0],
                 lru_conv=state_lru_conv[0])
    both = {k: s if k in ("ml_c", "ret") else jnp.concatenate([s, jnp.zeros((bp,) + s.shape[1:], s.dtype)], axis=0)
            for k, s in given.items()}
    tpos = jnp.arange(N_META, dtype=F32)
    pos_short = jnp.concatenate([jnp.broadcast_to((N_META + PAST_LEN) + tpos, (bs // SUBLANES, N_META)),
                                 jnp.broadcast_to(tpos, (bp // SUBLANES, N_META))], axis=0)
    y_short, st_short = _trunk(x_short, pos_short, _states_in(**both), w, N_META)
    y_sample = y_short[:bs]
    new_s = _states_out(st_short, slice(0, bs))

    nbs = bs // SUBLANES
    nbg_short = (bs + bp) // SUBLANES
    st_meta = {}
    for name, val in st_short.items():
        per_group = val.shape[0] // nbg_short
        st_meta[name] = val[nbs * per_group:]
    pos_long = jnp.broadcast_to(N_META + jnp.arange(tp, dtype=F32), (bp // SUBLANES, tp))
    y_prompt, st_long = _trunk(x_prompt, pos_long, st_meta, w, CHUNK)
    new_p = _states_out(st_long)

    return ((y_prompt, y_sample)
            + tuple(new_p[k][None] for k in _STATE_ORDER)
            + tuple(new_s[k][None] for k in _STATE_ORDER))
```

```python
import functools
import math

import numpy as np
import jax
import jax.numpy as jnp
from jax import lax
from jax.experimental import pallas as pl
from jax.experimental.pallas import tpu as pltpu

F32 = jnp.float32
BF16 = jnp.bfloat16

SUBLANES = 8
LANES = 128
MXU_DIM = 256
VMEM_LIMIT = 60 * 1024 * 1024

D_MODEL = 1024
CHUNK = 64
N_META = 16
PAST_LEN = 2048
EPS = 1e-6
CONV_K = 4
S5_GROUP = 16
S5_STATE = 64
W_A = D_MODEL // 2
G_A = W_A // S5_GROUP
W_S = G_A * S5_STATE
W_B = D_MODEL
H_B = 4
DH_B = W_B // H_B
W_C = D_MODEL
H_C = 4
DV_C = W_C // H_C
DK_C = DV_C // 2
QK_C = H_C * DK_C
ROPE_BASE = 10000.0
W_D = D_MODEL
H_D = 8
BD_D = W_D // H_D
LRU_C = 8.0
EV_IN = 2 * W_A + 2 * W_B
OD_IN = 2 * QK_C + 2 * W_C + 2 * W_D
HIST = (CONV_K - 1) * SUBLANES

ROW_TILE = 512
S5_COLS = 1024


def _cparams(sem):
    return pltpu.CompilerParams(dimension_semantics=sem, vmem_limit_bytes=VMEM_LIMIT)


def _const_spec(shape):
    nd = len(shape)
    return pl.BlockSpec(shape, lambda *_: (0,) * nd, pipeline_mode=pl.Buffered(1))


def _dot(a, b):
    return jnp.dot(a, b, preferred_element_type=F32)


def _dot_nt(a, b):
    return lax.dot_general(a, b, (((1,), (1,)), ((), ())), preferred_element_type=F32)


def _dot_tn(a, b):
    return lax.dot_general(a, b, (((0,), (0,)), ((), ())), preferred_element_type=F32)


def _rms(x, g):
    return x * lax.rsqrt(jnp.mean(x * x, axis=-1, keepdims=True) + EPS) * g


def _softplus(x):
    return jnp.maximum(x, 0.0) + jnp.log1p(jnp.exp(-jnp.abs(x)))


def _sigmoid(x):
    return 0.5 * jnp.tanh(0.5 * x) + 0.5


def _silu(x):
    return x * _sigmoid(x)


def _put_cols(dst3, col0, val):
    for j in range(val.shape[1] // LANES):
        dst3[col0 // LANES + j] = val[:, j * LANES:(j + 1) * LANES]


def _get_all(src3):
    return jnp.concatenate([src3[j] for j in range(src3.shape[0])], axis=-1)


def _get_batch(src3, b, tc, col0, width):
    parts = [src3[col0 // LANES + j, pl.ds(b, tc, stride=SUBLANES), :] for j in range(width // LANES)]
    return parts[0] if len(parts) == 1 else jnp.concatenate(parts, axis=-1)


def _put_batch(dst3, b, tc, col0, val):
    for j in range(val.shape[1] // LANES):
        dst3[col0 // LANES + j, pl.ds(b, tc, stride=SUBLANES), :] = val[:, j * LANES:(j + 1) * LANES]


def _get_batches(src3, tc, col0, width):
    return jnp.stack([_get_batch(src3, b, tc, col0, width) for b in range(SUBLANES)], axis=0)


def _put_batches(dst3, tc, col0, val):
    for b in range(SUBLANES):
        _put_batch(dst3, b, tc, col0, val[b])


def _bdot(a, b):
    return lax.dot_general(a, b, (((2,), (1,)), ((0,), (0,))), preferred_element_type=F32)


def _bdot_nt(a, b):
    return lax.dot_general(a, b, (((2,), (2,)), ((0,), (0,))), preferred_element_type=F32)


def _bdot_tn(a, b):
    return lax.dot_general(a, b, (((1,), (1,)), ((0,), (0,))), preferred_element_type=F32)


def _head_norm(h):
    mu = jnp.mean(h, axis=-1, keepdims=True)
    d = h - mu
    var = jnp.mean(d * d, axis=-1, keepdims=True)
    return d * lax.rsqrt(var + EPS)


def _load_bm_as_tm(x_ref, xs):
    tt = x_ref.shape[1]
    for b in range(SUBLANES):
        _put_batch(xs, b, tt, 0, x_ref[b])
    return _get_all(xs)


def _act_spec(bm, tt, width):
    if bm:
        return pl.BlockSpec((SUBLANES, tt, width), lambda g, i: (g, i, 0))
    return pl.BlockSpec((None, tt * SUBLANES, width), lambda g, i: (g, i, 0))


def _act_dims(x, bm):
    return (x.shape[0] // SUBLANES, x.shape[1]) if bm else (x.shape[0], x.shape[1] // SUBLANES)


def _proj_in_kernel(x_ref, g_ref, w_ref, o_ref, *scratch, x_bm):
    x = _load_bm_as_tm(x_ref, scratch[0]) if x_bm else x_ref[...]
    o_ref[...] = _dot(_rms(x, g_ref[...]).astype(BF16), w_ref[...])


def proj_in(x, g, w, x_bm):
    d, n = w.shape
    nbg, t = _act_dims(x, x_bm)
    tt = min(ROW_TILE // SUBLANES, t)
    rows = tt * SUBLANES
    return pl.pallas_call(
        functools.partial(_proj_in_kernel, x_bm=x_bm),
        grid=(nbg, t // tt),
        in_specs=[_act_spec(x_bm, tt, d), _const_spec((1, d)), _const_spec((d, n))],
        out_specs=_act_spec(False, tt, n),
        out_shape=jax.ShapeDtypeStruct((nbg, t * SUBLANES, n), F32),
        scratch_shapes=[pltpu.VMEM((d // LANES, rows, LANES), F32)] if x_bm else [],
        compiler_params=_cparams(("parallel", "parallel")),
        name="proj_in",
    )(x, g, w)


def _s5_prep_kernel(lre_ref, lim_ref, ldt_ref, bre_ref, bim_ref, are_ref, aim_ref, bbre_ref, bbim_ref):
    lre, lim = lre_ref[...], lim_ref[...]
    dt = jnp.exp(ldt_ref[...])
    mag = jnp.exp(lre * dt)
    ab_re = mag * jnp.cos(lim * dt)
    ab_im = mag * jnp.sin(lim * dt)
    den = lre * lre + lim * lim
    nr = ab_re - 1.0
    k_re = (nr * lre + ab_im * lim) / den
    k_im = (ab_im * lre - nr * lim) / den
    bre, bim = bre_ref[...], bim_ref[...]
    are_ref[...] = ab_re
    aim_ref[...] = ab_im
    bbre_ref[...] = k_re * bre - k_im * bim
    bbim_ref[...] = k_re * bim + k_im * bre


def s5_prep(lam_re, lam_im, log_dt, b_re, b_im):
    rep = lambda a: jnp.repeat(a, S5_GROUP, axis=0)
    rows = G_A * S5_GROUP
    ldt = jnp.broadcast_to(rep(log_dt[:, None]), (rows, S5_STATE))
    tr = lambda b: jnp.swapaxes(b, 1, 2).reshape(rows, S5_STATE)
    outs = pl.pallas_call(
        _s5_prep_kernel,
        out_shape=[jax.ShapeDtypeStruct((rows, S5_STATE), F32)] * 4,
        name="s5_prep",
    )(rep(lam_re), rep(lam_im), ldt, tr(b_re), tr(b_im))
    are, aim, bbre, bbim = (o.reshape(G_A, S5_GROUP, S5_STATE) for o in outs)
    return are[:, 0], aim[:, 0], bbre, bbim


def _block_diag(blocks, per):
    n, r, c = blocks.shape
    b = blocks.reshape(n // per, per, r, c)
    eye = jnp.eye(per, dtype=blocks.dtype)
    return jnp.einsum("hgrc,gk->hgrkc", b, eye).reshape(n // per, per * r, per * c)


def _s5_kernel(ua_ref, za_ref, h0re_ref, h0im_ref, are_ref, aim_ref, wbre_ref, wbim_ref,
               wcre_ref, wcim_ref, d_ref, wglu_ref, bglu_ref,
               ga_ref, sre_ref, sim_ref, bre_s, bim_s, *, ts):
    s = pl.program_id(1)

    @pl.when(s == 0)
    def _():
        sre_ref[...] = h0re_ref[...]
        sim_ref[...] = h0im_ref[...]

    ua = ua_ref[...]
    ua_bf = ua.astype(BF16)
    nhalf = W_A // MXU_DIM
    hw = W_S // nhalf
    for hf in range(nhalf):
        lhs = ua_bf[:, hf * MXU_DIM:(hf + 1) * MXU_DIM]
        bre_s[:, hf * hw:(hf + 1) * hw] = _dot(lhs, wbre_ref[hf])
        bim_s[:, hf * hw:(hf + 1) * hw] = _dot(lhs, wbim_ref[hf])

    for cb in range(W_S // S5_COLS):
        cols = slice(cb * S5_COLS, (cb + 1) * S5_COLS)
        ar = jnp.broadcast_to(are_ref[:, cols], (SUBLANES, S5_COLS))
        ai = jnp.broadcast_to(aim_ref[:, cols], (SUBLANES, S5_COLS))

        def step(t, carry, cols=cols, ar=ar, ai=ai):
            hr, hi = carry
            r0 = pl.multiple_of(t * SUBLANES, SUBLANES)
            nr = ar * hr - ai * hi + bre_s[pl.ds(r0, SUBLANES), cols]
            ni = ar * hi + ai * hr + bim_s[pl.ds(r0, SUBLANES), cols]
            bre_s[pl.ds(r0, SUBLANES), cols] = nr
            bim_s[pl.ds(r0, SUBLANES), cols] = ni
            return nr, ni

        hr, hi = lax.fori_loop(0, ts, step, (sre_ref[:, cols], sim_ref[:, cols]), unroll=4)
        sre_ref[:, cols] = hr
        sim_ref[:, cols] = hi

    ys = []
    for hf in range(nhalf):
        hre = bre_s[:, hf * hw:(hf + 1) * hw].astype(BF16)
        him = bim_s[:, hf * hw:(hf + 1) * hw].astype(BF16)
        ys.append(_dot(hre, wcre_ref[hf]) + _dot(him, wcim_ref[hf]))
    y = jnp.concatenate(ys, axis=-1) + d_ref[...] * ua
    y = jax.nn.gelu(y, approximate=True)
    y = y * _sigmoid(_dot(y.astype(BF16), wglu_ref[...]) + bglu_ref[...])
    ga_ref[...] = (y * _silu(za_ref[...])).astype(BF16)


def s5_mixer(u3, h0re, h0im, w, ts):
    nbg, rows, _ = u3.shape
    rs = ts * SUBLANES
    nsteps = rows // rs
    st_spec = pl.BlockSpec((None, SUBLANES, W_S), lambda g, s: (g, 0, 0))
    nhalf = W_A // MXU_DIM
    hw = W_S // nhalf
    return pl.pallas_call(
        functools.partial(_s5_kernel, ts=ts),
        grid=(nbg, nsteps),
        in_specs=[pl.BlockSpec((None, rs, W_A), lambda g, s: (g, s, 0)),
                  pl.BlockSpec((None, rs, W_A), lambda g, s: (g, s, 1)),
                  st_spec, st_spec,
                  _const_spec((1, W_S)), _const_spec((1, W_S)),
                  _const_spec((nhalf, MXU_DIM, hw)), _const_spec((nhalf, MXU_DIM, hw)),
                  _const_spec((nhalf, hw, MXU_DIM)), _const_spec((nhalf, hw, MXU_DIM)),
                  _const_spec((1, W_A)), _const_spec((W_A, W_A)), _const_spec((1, W_A))],
        out_specs=[pl.BlockSpec((None, rs, W_A), lambda g, s: (g, s, 0)), st_spec, st_spec],
        out_shape=[jax.ShapeDtypeStruct((nbg, rows, W_A), BF16),
                   jax.ShapeDtypeStruct((nbg, SUBLANES, W_S), F32),
                   jax.ShapeDtypeStruct((nbg, SUBLANES, W_S), F32)],
        scratch_shapes=[pltpu.VMEM((rs, W_S), F32), pltpu.VMEM((rs, W_S), F32)],
        compiler_params=_cparams(("parallel", "arbitrary")),
        name="s5_mixer",
    )(u3, u3, h0re, h0im, w["s5_are"], w["s5_aim"], w["s5_wbre"], w["s5_wbim"],
      w["s5_wcre"], w["s5_wcim"], w["s5_d"], w["s5_wglu"], w["s5_bglu"])


def _conv_history(c, buf0_ref, xpad, buf1_ref, rc):
    @pl.when(c == 0)
    def _():
        xpad[0:HIST, :] = buf0_ref[...]
        buf1_ref[...] = buf0_ref[...]

    @pl.when(c > 0)
    def _():
        xpad[0:HIST, :] = xpad[rc:rc + HIST, :]


def _conv_step(live, x_ref, xpad, cw_ref, cb_ref, buf1_ref, rc):
    xpad[HIST:HIST + rc, :] = x_ref[...]
    buf1_ref[...] = jnp.where(live, xpad[rc:rc + HIST, :], buf1_ref[...])
    out = cb_ref[...]
    for tap in range(CONV_K):
        out = out + xpad[tap * SUBLANES:tap * SUBLANES + rc, :] * cw_ref[tap:tap + 1, :]
    return out


def _mlstm_kernel(x_ref, ga_ref, xb_ref, zb_ref, buf0_ref, c0_hbm, n0_ref, m0_ref, cw_ref, cb_ref,
                  wq_ref, wk_ref, wv_ref, wif_ref, bif_ref, nw_ref, skip_ref, wa_ref, wb_ref,
                  h1_ref, c1_hbm, n1_ref, m1_ref, buf1_ref,
                  xpad, xc_s, q_s, k_s, v_s, g_s, h_s, c_s, gb_s, xs, sem, *, tc, n_given):
    g = pl.program_id(0)
    c = pl.program_id(1)
    nch = pl.num_programs(1) - 1
    live = c < nch
    rc = tc * SUBLANES

    def state_copy(h, to_hbm):
        hbm = (c1_hbm if to_hbm else c0_hbm).at[pl.ds(g * SUBLANES, SUBLANES), h]
        src, dst = (c_s.at[h], hbm) if to_hbm else (hbm, c_s.at[h])
        return pltpu.make_async_copy(src, dst, sem.at[h])

    @pl.when(c == 0)
    def _():
        n1_ref[...] = n0_ref[...]
        m1_ref[...] = m0_ref[...]
        gb_s[...] = jnp.zeros(gb_s.shape, BF16)

    @pl.when(c == nch)
    def _():
        for h in range(H_B):
            state_copy(h, True).start()
        for h in range(H_B):
            state_copy(h, True).wait()

    @pl.when((c == 0) & (g < n_given))
    def _():
        for h in range(H_B):
            state_copy(h, False).start()
        for h in range(H_B):
            state_copy(h, False).wait()

    @pl.when((c == 0) & (g >= n_given))
    def _():
        c_s[...] = jnp.zeros(c_s.shape, F32)

    _conv_history(c, buf0_ref, xpad, buf1_ref, rc)

    for b in range(SUBLANES):
        _put_batch(xs, b, tc, 0, x_ref[b])

    xc = _silu(_conv_step(live, xb_ref, xpad, cw_ref, cb_ref, buf1_ref, rc))
    xc_s[...] = xc
    xc_bf = xc.astype(BF16)
    xb_bf = xb_ref[...].astype(BF16)
    gates = bif_ref[...]
    for h in range(H_B):
        cols = slice(h * DH_B, (h + 1) * DH_B)
        for i, (src, w_ref, dst) in enumerate(((xc_bf, wq_ref, q_s), (xc_bf, wk_ref, k_s), (xb_bf, wv_ref, v_s))):
            r = _dot(src[:, cols], w_ref[h])
            _put_cols(dst, h * DH_B, r)
            gates = gates + _dot(r.astype(BF16), wif_ref[i * W_B + h * DH_B:i * W_B + (h + 1) * DH_B, :])
    lane = lax.broadcasted_iota(jnp.int32, (rc, LANES), 1)
    gates = jnp.where(lane < H_B, gates, -_softplus(-gates))
    is_f = lax.broadcasted_iota(jnp.int32, (SUBLANES, LANES), 1) >= H_B
    run = jnp.zeros((SUBLANES, LANES), F32)
    for t in range(tc):
        cur = gates[t * SUBLANES:(t + 1) * SUBLANES, :]
        run = jnp.where(is_f, run + cur, cur)
        g_s[0, t * SUBLANES:(t + 1) * SUBLANES, :] = run

    ri = lax.broadcasted_iota(jnp.int32, (tc, tc), 0)
    ci = lax.broadcasted_iota(jnp.int32, (tc, tc), 1)
    causal = ri >= ci
    eye = ri == ci
    m_lane = lax.broadcasted_iota(jnp.int32, (SUBLANES, 1, LANES), 2)
    kscale = DH_B ** -0.5

    gt = _get_batches(g_s, tc, 0, LANES)
    m_old = m1_ref[...]
    m_new = m_old
    for h in range(H_B):
        cols = slice(h * DH_B, (h + 1) * DH_B)
        y_a = _dot(ga_ref[...], wa_ref[:, cols])

        ig = gt[:, :, h:h + 1]
        bc = gt[:, :, H_B + h:H_B + h + 1]
        a_row = jnp.sum(jnp.where(eye, ig - bc, 0.0), axis=1, keepdims=True)
        dlog = jnp.where(causal, bc + a_row, -jnp.inf)
        inter = bc + m_old[:, :, h:h + 1]
        m = jnp.maximum(inter, jnp.max(dlog, axis=2, keepdims=True))
        p = jnp.exp(dlog - m)
        q = _get_batches(q_s, tc, h * DH_B, DH_B)
        k = _get_batches(k_s, tc, h * DH_B, DH_B) * kscale
        v_bf = _get_batches(v_s, tc, h * DH_B, DH_B).astype(BF16)
        q_bf = q.astype(BF16)
        sc = _bdot_nt(q_bf, k.astype(BF16)) * p
        w_inter = jnp.exp(inter - m)
        c0 = c_s[h]
        n0 = n1_ref[:, :, cols]
        num = _bdot(sc.astype(BF16), v_bf) + w_inter * _bdot(q_bf, c0.astype(BF16))
        den = jnp.sum(sc, axis=2, keepdims=True) + w_inter * jnp.sum(q * n0, axis=2, keepdims=True)
        hv = num * (1.0 / jnp.maximum(jnp.abs(den), jnp.exp(-m)))
        y_b = _dot(gb_s[...], wb_ref[:, cols])
        _put_batches(h_s, tc, h * DH_B, _head_norm(hv))
        m_end = m[:, tc - 1:tc, :]
        w_end = jnp.exp(bc[:, tc - 1:tc, :] - bc + ig - m_end)
        decay = jnp.exp(inter[:, tc - 1:tc, :] - m_end)
        kw = k * w_end
        c_s[h] = decay * c0 + _bdot_tn(kw.astype(BF16), v_bf)
        n1_ref[:, :, cols] = jnp.where(live, decay * n0 + jnp.sum(kw, axis=1, keepdims=True), n0)
        m_new = jnp.where(m_lane == h, m_end, m_new)
        res = jnp.concatenate([xs[j] for j in range(h * DH_B // LANES, (h + 1) * DH_B // LANES)], axis=-1)
        h1_ref[:, cols] = res + y_a + y_b
    m1_ref[...] = jnp.where(live, m_new, m_old)

    y = _get_all(h_s) * nw_ref[...] + skip_ref[...] * xc_s[...]
    gb_s[...] = (y * _silu(zb_ref[...])).astype(BF16)


def mlstm_mixer(x, ga, u3, buf0, c0, n0, m0, w, tc):
    nbg, rows, _ = u3.shape
    rc = tc * SUBLANES
    nch = rows // rc
    grp = lambda shape: pl.BlockSpec((None,) + shape, lambda g, c: (g, 0, 0))
    per_b = lambda width: pl.BlockSpec((SUBLANES, 1, width), lambda g, c: (g, 0, 0))
    cur = lambda blk: (lambda g, c: (g, jnp.minimum(c, nch - 1), blk))
    prev = lambda g, c: (g, jnp.maximum(c - 1, 0), 0)
    blocked = pltpu.VMEM((W_B // LANES, rc, LANES), F32)
    n_given = c0.shape[0] // SUBLANES
    return pl.pallas_call(
        functools.partial(_mlstm_kernel, tc=tc, n_given=n_given),
        grid=(nbg, nch + 1),
        in_specs=[pl.BlockSpec((SUBLANES, tc, D_MODEL), prev),
                  pl.BlockSpec((None, rc, W_A), prev),
                  pl.BlockSpec((None, rc, W_B), cur(1)),
                  pl.BlockSpec((None, rc, W_B), cur(2)),
                  grp((HIST, W_B)),
                  pl.BlockSpec(memory_space=pl.ANY),
                  per_b(W_B), per_b(LANES),
                  _const_spec((CONV_K, W_B)), _const_spec((1, W_B)),
                  _const_spec((H_B, DH_B, DH_B)), _const_spec((H_B, DH_B, DH_B)), _const_spec((H_B, DH_B, DH_B)),
                  _const_spec((3 * W_B, LANES)), _const_spec((1, LANES)),
                  _const_spec((1, W_B)), _const_spec((1, W_B)),
                  _const_spec((W_A, D_MODEL)), _const_spec((W_B, D_MODEL))],
        out_specs=[pl.BlockSpec((None, rc, D_MODEL), prev),
                   pl.BlockSpec(memory_space=pl.ANY),
                   per_b(W_B), per_b(LANES), grp((HIST, W_B))],
        out_shape=[jax.ShapeDtypeStruct((nbg, rows, D_MODEL), F32),
                   jax.ShapeDtypeStruct((nbg * SUBLANES,) + c0.shape[1:], F32),
                   jax.ShapeDtypeStruct((nbg * SUBLANES, 1, W_B), F32),
                   jax.ShapeDtypeStruct((nbg * SUBLANES, 1, LANES), F32),
                   jax.ShapeDtypeStruct((nbg, HIST, W_B), F32)],
        scratch_shapes=[pltpu.VMEM((rc + HIST, W_B), F32),
                        pltpu.VMEM((rc, W_B), F32), blocked, blocked, blocked,
                        pltpu.VMEM((1, rc, LANES), F32), blocked,
                        pltpu.VMEM((H_B, SUBLANES, DH_B, DH_B), F32),
                        pltpu.VMEM((rc, W_B), BF16), blocked,
                        pltpu.SemaphoreType.DMA((H_B,))],
        compiler_params=_cparams(("parallel", "arbitrary")),
        name="mlstm_mixer",
    )(x, ga, u3, u3, buf0, c0, n0, m0, w["ml_cw"], w["ml_cb"], w["ml_wq"], w["ml_wk"], w["ml_wv"],
      w["ml_wif"], w["ml_bif"], w["ml_nw"], w["ml_skip"], w["ev_w_out_a"], w["ev_w_out_b"])


def _ret_log_decay(h):
    return float(np.log1p(-np.exp2(-5.0 - h)))


OD_MIX_IN = 2 * QK_C + 2 * W_C


def _retention_kernel(h_ref, g1_ref, w_ref, cos_ref, sin_ref, s0_hbm, nw_ref,
                      gc_ref, s1_hbm, ud_ref, hn_s, slot, zc_s, q_s, k_s, v_s, o_s, st_s, sem, *, tc, n_given):
    g = pl.program_id(0)
    c = pl.program_id(1)
    started = c > 0

    @pl.when(c == 0)
    def _():
        slot[...] = jnp.zeros(slot.shape, F32)

    def state_copy(h, to_hbm):
        hbm = (s1_hbm if to_hbm else s0_hbm).at[pl.ds(g * SUBLANES, SUBLANES), h]
        src, dst = (st_s.at[h], hbm) if to_hbm else (hbm, st_s.at[h])
        return pltpu.make_async_copy(src, dst, sem.at[h])

    @pl.when((c == 0) & (g < n_given))
    def _():
        for h in range(H_C):
            state_copy(h, False).start()
        for h in range(H_C):
            state_copy(h, False).wait()

    @pl.when((c == 0) & (g >= n_given))
    def _():
        st_s[...] = jnp.zeros(st_s.shape, F32)

    cos, sin = cos_ref[...], sin_ref[...]
    _put_cols(v_s, 0, slot[:, 2 * QK_C:2 * QK_C + W_C])
    zc_s[...] = slot[:, 2 * QK_C + W_C:OD_MIX_IN]
    for h in range(H_C):
        for off, dst, scale in ((0, q_s, None), (QK_C, k_s, DK_C ** -0.5)):
            x = slot[:, off + h * DK_C:off + (h + 1) * DK_C]
            r = x * cos + pltpu.roll(x, DK_C // 2, 1) * sin
            dst[h] = r if scale is None else r * scale

    hn_s[...] = _rms(h_ref[...], g1_ref[...]).astype(BF16)

    def project(j):
        cols = slice(j * MXU_DIM, (j + 1) * MXU_DIM)
        r = _dot(hn_s[...], w_ref[:, cols])
        if (j + 1) * MXU_DIM <= OD_MIX_IN:
            slot[:, cols] = r
        else:
            ud_ref[:, j * MXU_DIM - OD_MIX_IN:(j + 1) * MXU_DIM - OD_MIX_IN] = r

    per_head = OD_IN // MXU_DIM // H_C

    ri = lax.broadcasted_iota(jnp.int32, (tc, tc), 0)
    ci = lax.broadcasted_iota(jnp.int32, (tc, tc), 1)
    diff = (ri - ci).astype(F32)
    tcol = lax.broadcasted_iota(jnp.int32, (tc, 1), 0).astype(F32)
    dmask, xi, zeta, gfull = [], [], [], []
    for h in range(H_C):
        lg = _ret_log_decay(h)
        dmask.append(jnp.where(diff >= 0, jnp.exp(lg * jnp.maximum(diff, 0.0)), 0.0))
        xi.append(jnp.exp(lg * (tcol + 1.0)))
        zeta.append(jnp.exp(lg * (tc - 1.0 - tcol)))
        gfull.append(float(np.exp(np.float32(lg) * np.float32(tc))))

    for h in range(H_C):
        nxt = iter(range(h * per_head, (h + 1) * per_head))
        project(next(nxt))
        q_bf = _get_batches(q_s, tc, h * DK_C, DK_C).astype(BF16)
        k = _get_batches(k_s, tc, h * DK_C, DK_C)
        v_bf = _get_batches(v_s, tc, h * DV_C, DV_C).astype(BF16)
        s0 = st_s[h]
        inner = _bdot_nt(q_bf, k.astype(BF16)) * dmask[h]
        project(next(nxt))
        o = _bdot(inner.astype(BF16), v_bf) + xi[h] * _bdot(q_bf, s0.astype(BF16))
        project(next(nxt))
        _put_batches(o_s, tc, h * DV_C, _head_norm(o))
        project(next(nxt))
        decay = jnp.where(started, gfull[h], 1.0)
        st_s[h] = decay * s0 + _bdot_tn((k * zeta[h]).astype(BF16), v_bf)
        for j in nxt:
            project(j)

    gc_ref[...] = (_get_all(o_s) * nw_ref[...] * _silu(zc_s[...])).astype(BF16)

    @pl.when(c == pl.num_programs(1) - 1)
    def _():
        for h in range(H_C):
            state_copy(h, True).start()
        for h in range(H_C):
            state_copy(h, True).wait()


def retention_mixer(h1, cos, sin, s0, w, tc):
    nbg, rows, d = h1.shape
    rc = tc * SUBLANES
    nch = rows // rc
    assert (OD_IN // MXU_DIM) % H_C == 0 and OD_MIX_IN % MXU_DIM == 0
    n_given = s0.shape[0] // SUBLANES
    cur = lambda g, c: (g, jnp.minimum(c, nch - 1), 0)
    prev = lambda g, c: (g, jnp.maximum(c - 1, 0), 0)
    return pl.pallas_call(
        functools.partial(_retention_kernel, tc=tc, n_given=n_given),
        grid=(nbg, nch + 1),
        in_specs=[pl.BlockSpec((None, rc, d), cur),
                  _const_spec((1, d)), _const_spec((d, OD_IN)),
                  pl.BlockSpec((None, rc, DK_C), prev),
                  pl.BlockSpec((None, rc, DK_C), prev),
                  pl.BlockSpec(memory_space=pl.ANY),
                  _const_spec((1, W_C))],
        out_specs=[pl.BlockSpec((None, rc, W_C), prev),
                   pl.BlockSpec(memory_space=pl.ANY),
                   pl.BlockSpec((None, rc, OD_IN - OD_MIX_IN), cur)],
        out_shape=[jax.ShapeDtypeStruct((nbg, rows, W_C), BF16),
                   jax.ShapeDtypeStruct((nbg * SUBLANES,) + s0.shape[1:], F32),
                   jax.ShapeDtypeStruct((nbg, rows, OD_IN - OD_MIX_IN), F32)],
        scratch_shapes=[pltpu.VMEM((rc, d), BF16), pltpu.VMEM((rc, OD_MIX_IN), F32), pltpu.VMEM((rc, W_C), F32),
                        pltpu.VMEM((QK_C // LANES, rc, LANES), F32), pltpu.VMEM((QK_C // LANES, rc, LANES), F32),
                        pltpu.VMEM((W_C // LANES, rc, LANES), F32), pltpu.VMEM((W_C // LANES, rc, LANES), F32),
                        pltpu.VMEM((H_C, SUBLANES, DK_C, DV_C), F32),
                        pltpu.SemaphoreType.DMA((H_C,))],
        compiler_params=_cparams(("parallel", "arbitrary")),
        name="retention_mixer",
    )(h1, w["norm1"], w["od_w_in"], cos, sin, s0, w["ret_nw"])


def _rglru_kernel(res_ref, gc_ref, xd_ref, zd_ref, buf0_ref, h0_ref, cw_ref, cb_ref, wa_ref, ba_ref, wx_ref,
                  bx_ref, lam_ref, woc_ref, wod_ref, fn_ref,
                  y_ref, h1_ref, buf1_ref, xpad, a_s, b_s, gd_s, ys, inv_s, *, tc):
    c = pl.program_id(1)
    nch = pl.num_programs(1) - 1
    live = c < nch
    rc = tc * SUBLANES

    @pl.when(c == 0)
    def _():
        h1_ref[...] = h0_ref[...]
        gd_s[...] = jnp.zeros(gd_s.shape, BF16)

    _conv_history(c, buf0_ref, xpad, buf1_ref, rc)

    xpad[HIST:HIST + rc, :] = xd_ref[...]
    buf1_ref[...] = jnp.where(live, xpad[rc:rc + HIST, :], buf1_ref[...])
    sp = _softplus(-lam_ref[...])
    ssq = jnp.zeros((rc, 1), F32)
    for j in range(W_D // MXU_DIM):
        cols = slice(j * MXU_DIM, (j + 1) * MXU_DIM)
        y_c = _dot(gc_ref[...], woc_ref[:, cols])
        xc = cb_ref[:, cols]
        for tap in range(CONV_K):
            xc = xc + xpad[tap * SUBLANES:tap * SUBLANES + rc, cols] * cw_ref[tap:tap + 1, cols]
        lhs = xc.astype(BF16)
        r = _sigmoid(_dot(lhs, wa_ref[j]) + ba_ref[:, cols])
        y_d = _dot(gd_s[...], wod_ref[:, cols])
        i = _sigmoid(_dot(lhs, wx_ref[j]) + bx_ref[:, cols])
        log_a = -LRU_C * r * sp[:, cols]
        a = jnp.exp(log_a)
        a_s[:, cols] = a
        b_s[:, cols] = jnp.sqrt(-jnp.tanh(log_a) * (a * a + 1.0)) * (i * xc)
        yj = res_ref[:, cols] + y_c + y_d
        _put_cols(ys, j * MXU_DIM, yj)
        ssq = ssq + jnp.sum(yj * yj, axis=-1, keepdims=True)

    inv_s[0] = jnp.broadcast_to(lax.rsqrt(ssq * (1.0 / D_MODEL) + EPS), (rc, LANES))
    for b in range(SUBLANES):
        inv_b = _get_batch(inv_s, b, tc, 0, LANES)[:, 0:1]
        y_ref[b] = _get_batch(ys, b, tc, 0, D_MODEL) * inv_b * fn_ref[...]

    def step(t, h):
        r0 = pl.multiple_of(t * SUBLANES, SUBLANES)
        h = a_s[pl.ds(r0, SUBLANES), :] * h + b_s[pl.ds(r0, SUBLANES), :]
        b_s[pl.ds(r0, SUBLANES), :] = h
        return h

    h_old = h1_ref[...]
    h1_ref[...] = jnp.where(live, lax.fori_loop(0, tc, step, h_old, unroll=4), h_old)
    gd_s[...] = (b_s[...] * _silu(zd_ref[...])).astype(BF16)


def rglru_mixer(res, gc, u3, buf0, h0, w, tc):
    nbg, rows, _ = u3.shape
    rc = tc * SUBLANES
    nch = rows // rc
    nblk = W_D // MXU_DIM
    grp = lambda shape: pl.BlockSpec((None,) + shape, lambda g, c: (g, 0, 0))
    cur = lambda blk: (lambda g, c: (g, jnp.minimum(c, nch - 1), blk))
    prev = lambda g, c: (g, jnp.maximum(c - 1, 0), 0)
    return pl.pallas_call(
        functools.partial(_rglru_kernel, tc=tc),
        grid=(nbg, nch + 1),
        in_specs=[pl.BlockSpec((None, rc, D_MODEL), prev),
                  pl.BlockSpec((None, rc, W_C), prev),
                  pl.BlockSpec((None, rc, W_D), cur(0)),
                  pl.BlockSpec((None, rc, W_D), cur(1)),
                  grp((HIST, W_D)), grp((SUBLANES, W_D)),
                  _const_spec((CONV_K, W_D)), _const_spec((1, W_D)),
                  _const_spec((nblk, MXU_DIM, MXU_DIM)), _const_spec((1, W_D)),
                  _const_spec((nblk, MXU_DIM, MXU_DIM)), _const_spec((1, W_D)),
                  _const_spec((1, W_D)),
                  _const_spec((W_C, D_MODEL)), _const_spec((W_D, D_MODEL)), _const_spec((1, D_MODEL))],
        out_specs=[pl.BlockSpec((SUBLANES, tc, D_MODEL), prev),
                   grp((SUBLANES, W_D)), grp((HIST, W_D))],
        out_shape=[jax.ShapeDtypeStruct((nbg * SUBLANES, rows // SUBLANES, D_MODEL), F32),
                   jax.ShapeDtypeStruct((nbg, SUBLANES, W_D), F32),
                   jax.ShapeDtypeStruct((nbg, HIST, W_D), F32)],
        scratch_shapes=[pltpu.VMEM((rc + HIST, W_D), F32), pltpu.VMEM((rc, W_D), F32), pltpu.VMEM((rc, W_D), F32),
                        pltpu.VMEM((rc, W_D), BF16), pltpu.VMEM((D_MODEL // LANES, rc, LANES), F32),
                        pltpu.VMEM((1, rc, LANES), F32)],
        compiler_params=_cparams(("parallel", "arbitrary")),
        name="rglru_mixer",
    )(res, gc, u3, u3, buf0, h0, w["lru_cw"], w["lru_cb"], w["lru_wa"], w["lru_ba"], w["lru_wx"], w["lru_bx"],
      w["lru_lam"], w["od_w_out_c"], w["od_w_out_d"], w["final_norm"])


def _to_tm(x):
    b, t, c = x.shape
    return x.reshape(b // SUBLANES, SUBLANES, t, c).swapaxes(1, 2).reshape(b // SUBLANES, t * SUBLANES, c)


def _from_tm(x, t):
    nbg, _, c = x.shape
    return x.reshape(nbg, t, SUBLANES, c).swapaxes(1, 2).reshape(nbg * SUBLANES, t, c)


def _rope_tables(pos):
    half = DK_C // 2
    inv = ROPE_BASE ** (-jnp.arange(half, dtype=F32) / half)
    ang = pos[..., None] * inv
    cos = jnp.cos(ang)
    sin = jnp.sin(ang)
    cos = jnp.concatenate([cos, cos], axis=-1)
    sin = jnp.concatenate([-sin, sin], axis=-1)
    rep = lambda a: jnp.repeat(a, SUBLANES, axis=1)
    return rep(cos), rep(sin)


def _prep_weights(p):
    w = {}
    w["norm0"] = p["norm_w"][0][None]
    w["norm1"] = p["norm_w"][1][None]
    w["final_norm"] = p["final_norm_w"][None]
    w["ev_w_in"] = p["ev_w_in"][0].astype(BF16)
    w["ev_w_out_a"] = p["ev_w_out"][0][:W_A].astype(BF16)
    w["ev_w_out_b"] = p["ev_w_out"][0][W_A:].astype(BF16)
    are, aim, bbre, bbim = s5_prep(p["s5_lambda_re"][0], p["s5_lambda_im"][0], p["s5_log_dt"][0],
                                   p["s5_b_re"][0], p["s5_b_im"][0])
    per = MXU_DIM // S5_GROUP
    w["s5_are"] = are.reshape(1, W_S)
    w["s5_aim"] = aim.reshape(1, W_S)
    w["s5_wbre"] = _block_diag(bbre, per).astype(BF16)
    w["s5_wbim"] = _block_diag(bbim, per).astype(BF16)
    w["s5_wcre"] = _block_diag(jnp.swapaxes(p["s5_c_re"][0], 1, 2), per).astype(BF16)
    w["s5_wcim"] = _block_diag(-jnp.swapaxes(p["s5_c_im"][0], 1, 2), per).astype(BF16)
    w["s5_d"] = p["s5_d"][0][None]
    w["s5_wglu"] = p["s5_w_glu"][0].astype(BF16)
    w["s5_bglu"] = p["s5_b_glu"][0][None]
    w["ml_cw"] = p["ml_conv_w"][0]
    w["ml_cb"] = p["ml_conv_b"][0][None]
    w["ml_wq"] = p["ml_wq"][0].astype(BF16)
    w["ml_wk"] = p["ml_wk"][0].astype(BF16)
    w["ml_wv"] = p["ml_wv"][0].astype(BF16)
    w["ml_wif"] = jnp.pad(p["ml_w_if"][0], ((0, 0), (0, LANES - 2 * H_B))).astype(BF16)
    w["ml_bif"] = jnp.pad(p["ml_b_if"][0], (0, LANES - 2 * H_B))[None]
    w["ml_nw"] = p["ml_norm_w"][0][None]
    w["ml_skip"] = p["ml_skip"][0][None]
    w["od_w_in"] = p["od_w_in"][0].astype(BF16)
    w["od_w_out_c"] = p["od_w_out"][0][:W_C].astype(BF16)
    w["od_w_out_d"] = p["od_w_out"][0][W_C:].astype(BF16)
    w["ret_nw"] = p["ret_norm_w"][0][None]
    w["lru_cw"] = p["lru_conv_w"][0]
    w["lru_cb"] = p["lru_conv_b"][0][None]
    w["lru_wa"] = _block_diag(p["lru_w_a"][0], MXU_DIM // BD_D).astype(BF16)
    w["lru_ba"] = p["lru_b_a"][0][None]
    w["lru_wx"] = _block_diag(p["lru_w_x"][0], MXU_DIM // BD_D).astype(BF16)
    w["lru_bx"] = p["lru_b_x"][0][None]
    w["lru_lam"] = p["lru_lambda"][0][None]
    return w


def _conv_to_tm(buf):
    return _to_tm(buf)


def _trunk(x, pos, st, w, tc):
    new = {}
    u = proj_in(x, w["norm0"], w["ev_w_in"], True)
    ga, new["s5_re"], new["s5_im"] = s5_mixer(u, st["s5_re"], st["s5_im"], w, tc)
    h1, new["ml_c"], new["ml_n"], new["ml_m"], new["ml_conv"] = mlstm_mixer(
        x, ga, u, st["ml_conv"], st["ml_c"], st["ml_n"], st["ml_m"], w, tc)
    cos, sin = _rope_tables(pos)
    gc, new["ret"], ud = retention_mixer(h1, cos, sin, st["ret"], w, tc)
    y, new["lru_h"], new["lru_conv"] = rglru_mixer(h1, gc, ud, st["lru_conv"], st["lru_h"], w, tc)
    return y, new


def _states_in(s5_re, s5_im, ml_c, ml_n, ml_m, ml_conv, ret, lru_h, lru_conv):
    b = s5_re.shape[0]
    nbg = b // SUBLANES
    return dict(
        s5_re=s5_re.reshape(nbg, SUBLANES, W_S),
        s5_im=s5_im.reshape(nbg, SUBLANES, W_S),
        ml_c=ml_c,
        ml_n=ml_n.reshape(b, 1, W_B),
        ml_m=jnp.pad(ml_m, ((0, 0), (0, LANES - H_B))).reshape(b, 1, LANES),
        ml_conv=_conv_to_tm(ml_conv),
        ret=ret,
        lru_h=lru_h.reshape(nbg, SUBLANES, W_D),
        lru_conv=_conv_to_tm(lru_conv),
    )


def _states_out(st, sel=None):
    nbg = st["s5_re"].shape[0]
    b = nbg * SUBLANES
    out = dict(
        s5_re=st["s5_re"].reshape(b, G_A, S5_STATE),
        s5_im=st["s5_im"].reshape(b, G_A, S5_STATE),
        ml_c=st["ml_c"].reshape(b, H_B, DH_B, DH_B),
        ml_n=st["ml_n"].reshape(b, H_B, DH_B),
        ml_m=st["ml_m"].reshape(b, LANES)[:, :H_B],
        ml_conv=_from_tm(st["ml_conv"], CONV_K - 1),
        ret=st["ret"].reshape(b, H_C, DK_C, DV_C),
        lru_h=st["lru_h"].reshape(b, W_D),
        lru_conv=_from_tm(st["lru_conv"], CONV_K - 1),
    )
    if sel is not None:
        out = {k: v[sel] for k, v in out.items()}
    return out


_STATE_ORDER = ("s5_re", "s5_im", "ml_c", "ml_n", "ml_m", "ml_conv", "ret", "lru_h", "lru_conv")


def kernel(x_prompt, x_sample, state_s5_re, state_s5_im, state_ml_c, state_ml_n, state_ml_m, state_ml_conv, state_ret, state_lru_h, state_lru_conv, meta, norm_w, final_norm_w, ev_w_in, ev_w_out, s5_lambda_re, s5_lambda_im, s5_log_dt, s5_b_re, s5_b_im, s5_c_re, s5_c_im, s5_d, s5_w_glu, s5_b_glu, ml_conv_w, ml_conv_b, ml_wq, ml_wk, ml_wv, ml_w_if, ml_b_if, ml_norm_w, ml_skip, od_w_in, od_w_out, ret_norm_w, lru_conv_w, lru_conv_b, lru_w_a, lru_b_a, lru_w_x, lru_b_x, lru_lambda):
    p = dict(norm_w=norm_w, final_norm_w=final_norm_w, ev_w_in=ev_w_in, ev_w_out=ev_w_out,
             s5_lambda_re=s5_lambda_re, s5_lambda_im=s5_lambda_im, s5_log_dt=s5_log_dt,
             s5_b_re=s5_b_re, s5_b_im=s5_b_im, s5_c_re=s5_c_re, s5_c_im=s5_c_im, s5_d=s5_d,
             s5_w_glu=s5_w_glu, s5_b_glu=s5_b_glu, ml_conv_w=ml_conv_w, ml_conv_b=ml_conv_b,
             ml_wq=ml_wq, ml_wk=ml_wk, ml_wv=ml_wv, ml_w_if=ml_w_if, ml_b_if=ml_b_if,
             ml_norm_w=ml_norm_w, ml_skip=ml_skip, od_w_in=od_w_in, od_w_out=od_w_out,
             ret_norm_w=ret_norm_w, lru_conv_w=lru_conv_w, lru_conv_b=lru_conv_b,
             lru_w_a=lru_w_a, lru_b_a=lru_b_a, lru_w_x=lru_w_x, lru_b_x=lru_b_x, lru_lambda=lru_lambda)
    assert ev_w_in.shape[0] == 1 and od_w_in.shape[0] == 1, "two-layer trunk"
    w = _prep_weights(p)
    bp, tp, _ = x_prompt.shape
    bs, ts, _ = x_sample.shape
    assert ts == N_META and bp % SUBLANES == 0 and bs % SUBLANES == 0 and tp % CHUNK == 0

    meta_b = jnp.broadcast_to(meta[None], (bp, N_META, D_MODEL))
    x_short = jnp.concatenate([x_sample, meta_b], axis=0)
    given = dict(s5_re=state_s5_re[0], s5_im=state_s5_im[0], ml_c=state_ml_c[0], ml_n=state_ml_n[0],
                 ml_m=state_ml_m[0], ml_conv=state_ml_conv[0], ret=state_ret[0], lru_h=state_lru_h[0],
                 lru_conv=state_lru_conv[0])
    both = {k: s if k in ("ml_c", "ret") else jnp.concatenate([s, jnp.zeros((bp,) + s.shape[1:], s.dtype)], axis=0)
            for k, s in given.items()}
    tpos = jnp.arange(N_META, dtype=F32)
    pos_short = jnp.concatenate([jnp.broadcast_to((N_META + PAST_LEN) + tpos, (bs // SUBLANES, N_META)),
                                 jnp.broadcast_to(tpos, (bp // SUBLANES, N_META))], axis=0)
    y_short, st_short = _trunk(x_short, pos_short, _states_in(**both), w, N_META)
    y_sample = y_short[:bs]
    new_s = _states_out(st_short, slice(0, bs))

    nbs = bs // SUBLANES
    nbg_short = (bs + bp) // SUBLANES
    st_meta = {}
    for name, val in st_short.items():
        per_group = val.shape[0] // nbg_short
        st_meta[name] = val[nbs * per_group:]
    pos_long = jnp.broadcast_to(N_META + jnp.arange(tp, dtype=F32), (bp // SUBLANES, tp))
    y_prompt, st_long = _trunk(x_prompt, pos_long, st_meta, w, CHUNK)
    new_p = _states_out(st_long)

    return ((y_prompt, y_sample)
            + tuple(new_p[k][None] for k in _STATE_ORDER)
            + tuple(new_s[k][None] for k in _STATE_ORDER))
```

```python
import functools
import math

import numpy as np
import jax
import jax.numpy as jnp
from jax import lax
from jax.experimental import pallas as pl
from jax.experimental.pallas import tpu as pltpu

F32 = jnp.float32
BF16 = jnp.bfloat16

SUBLANES = 8
LANES = 128
MXU_DIM = 256
VMEM_LIMIT = 60 * 1024 * 1024

D_MODEL = 1024
CHUNK = 64
N_META = 16
PAST_LEN = 2048
EPS = 1e-6
CONV_K = 4
S5_GROUP = 16
S5_STATE = 64
W_A = D_MODEL // 2
G_A = W_A // S5_GROUP
W_S = G_A * S5_STATE
W_B = D_MODEL
H_B = 4
DH_B = W_B // H_B
W_C = D_MODEL
H_C = 4
DV_C = W_C // H_C
DK_C = DV_C // 2
QK_C = H_C * DK_C
ROPE_BASE = 10000.0
W_D = D_MODEL
H_D = 8
BD_D = W_D // H_D
LRU_C = 8.0
EV_IN = 2 * W_A + 2 * W_B
OD_IN = 2 * QK_C + 2 * W_C + 2 * W_D
HIST = (CONV_K - 1) * SUBLANES

ROW_TILE = 512
S5_COLS = 512


def _cparams(sem):
    return pltpu.CompilerParams(dimension_semantics=sem, vmem_limit_bytes=VMEM_LIMIT)


def _const_spec(shape):
    nd = len(shape)
    return pl.BlockSpec(shape, lambda *_: (0,) * nd, pipeline_mode=pl.Buffered(1))


def _dot(a, b):
    return jnp.dot(a, b, preferred_element_type=F32)


def _dot_nt(a, b):
    return lax.dot_general(a, b, (((1,), (1,)), ((), ())), preferred_element_type=F32)


def _dot_tn(a, b):
    return lax.dot_general(a, b, (((0,), (0,)), ((), ())), preferred_element_type=F32)


def _rms(x, g):
    return x * lax.rsqrt(jnp.mean(x * x, axis=-1, keepdims=True) + EPS) * g


def _softplus(x):
    return jnp.maximum(x, 0.0) + jnp.log1p(jnp.exp(-jnp.abs(x)))


def _sigmoid(x):
    return 0.5 * jnp.tanh(0.5 * x) + 0.5


def _silu(x):
    return x * _sigmoid(x)


def _put_cols(dst3, col0, val):
    for j in range(val.shape[1] // LANES):
        dst3[col0 // LANES + j] = val[:, j * LANES:(j + 1) * LANES]


def _get_all(src3):
    return jnp.concatenate([src3[j] for j in range(src3.shape[0])], axis=-1)


def _get_batch(src3, b, tc, col0, width):
    parts = [src3[col0 // LANES + j, pl.ds(b, tc, stride=SUBLANES), :] for j in range(width // LANES)]
    return parts[0] if len(parts) == 1 else jnp.concatenate(parts, axis=-1)


def _put_batch(dst3, b, tc, col0, val):
    for j in range(val.shape[1] // LANES):
        dst3[col0 // LANES + j, pl.ds(b, tc, stride=SUBLANES), :] = val[:, j * LANES:(j + 1) * LANES]


def _get_batches(src3, tc, col0, width):
    return jnp.stack([_get_batch(src3, b, tc, col0, width) for b in range(SUBLANES)], axis=0)


def _put_batches(dst3, tc, col0, val):
    for b in range(SUBLANES):
        _put_batch(dst3, b, tc, col0, val[b])


def _bdot(a, b):
    return lax.dot_general(a, b, (((2,), (1,)), ((0,), (0,))), preferred_element_type=F32)


def _bdot_nt(a, b):
    return lax.dot_general(a, b, (((2,), (2,)), ((0,), (0,))), preferred_element_type=F32)


def _bdot_tn(a, b):
    return lax.dot_general(a, b, (((1,), (1,)), ((0,), (0,))), preferred_element_type=F32)


def _head_norm(h):
    mu = jnp.mean(h, axis=-1, keepdims=True)
    d = h - mu
    var = jnp.mean(d * d, axis=-1, keepdims=True)
    return d * lax.rsqrt(var + EPS)


def _load_bm_as_tm(x_ref, xs):
    tt = x_ref.shape[1]
    for b in range(SUBLANES):
        _put_batch(xs, b, tt, 0, x_ref[b])
    return _get_all(xs)


def _act_spec(bm, tt, width):
    if bm:
        return pl.BlockSpec((SUBLANES, tt, width), lambda g, i: (g, i, 0))
    return pl.BlockSpec((None, tt * SUBLANES, width), lambda g, i: (g, i, 0))


def _act_dims(x, bm):
    return (x.shape[0] // SUBLANES, x.shape[1]) if bm else (x.shape[0], x.shape[1] // SUBLANES)


def _proj_in_kernel(x_ref, g_ref, w_ref, o_ref, *scratch, x_bm):
    x = _load_bm_as_tm(x_ref, scratch[0]) if x_bm else x_ref[...]
    o_ref[...] = _dot(_rms(x, g_ref[...]).astype(BF16), w_ref[...])


def proj_in(x, g, w, x_bm):
    d, n = w.shape
    nbg, t = _act_dims(x, x_bm)
    tt = min(ROW_TILE // SUBLANES, t)
    rows = tt * SUBLANES
    return pl.pallas_call(
        functools.partial(_proj_in_kernel, x_bm=x_bm),
        grid=(nbg, t // tt),
        in_specs=[_act_spec(x_bm, tt, d), _const_spec((1, d)), _const_spec((d, n))],
        out_specs=_act_spec(False, tt, n),
        out_shape=jax.ShapeDtypeStruct((nbg, t * SUBLANES, n), F32),
        scratch_shapes=[pltpu.VMEM((d // LANES, rows, LANES), F32)] if x_bm else [],
        compiler_params=_cparams(("parallel", "parallel")),
        name="proj_in",
    )(x, g, w)


def _s5_prep_kernel(lre_ref, lim_ref, ldt_ref, bre_ref, bim_ref, are_ref, aim_ref, bbre_ref, bbim_ref):
    lre, lim = lre_ref[...], lim_ref[...]
    dt = jnp.exp(ldt_ref[...])
    mag = jnp.exp(lre * dt)
    ab_re = mag * jnp.cos(lim * dt)
    ab_im = mag * jnp.sin(lim * dt)
    den = lre * lre + lim * lim
    nr = ab_re - 1.0
    k_re = (nr * lre + ab_im * lim) / den
    k_im = (ab_im * lre - nr * lim) / den
    bre, bim = bre_ref[...], bim_ref[...]
    are_ref[...] = ab_re
    aim_ref[...] = ab_im
    bbre_ref[...] = k_re * bre - k_im * bim
    bbim_ref[...] = k_re * bim + k_im * bre


def s5_prep(lam_re, lam_im, log_dt, b_re, b_im):
    rep = lambda a: jnp.repeat(a, S5_GROUP, axis=0)
    rows = G_A * S5_GROUP
    ldt = jnp.broadcast_to(rep(log_dt[:, None]), (rows, S5_STATE))
    tr = lambda b: jnp.swapaxes(b, 1, 2).reshape(rows, S5_STATE)
    outs = pl.pallas_call(
        _s5_prep_kernel,
        out_shape=[jax.ShapeDtypeStruct((rows, S5_STATE), F32)] * 4,
        name="s5_prep",
    )(rep(lam_re), rep(lam_im), ldt, tr(b_re), tr(b_im))
    are, aim, bbre, bbim = (o.reshape(G_A, S5_GROUP, S5_STATE) for o in outs)
    return are[:, 0], aim[:, 0], bbre, bbim


def _block_diag(blocks, per):
    n, r, c = blocks.shape
    b = blocks.reshape(n // per, per, r, c)
    eye = jnp.eye(per, dtype=blocks.dtype)
    return jnp.einsum("hgrc,gk->hgrkc", b, eye).reshape(n // per, per * r, per * c)


def _s5_kernel(ua_ref, za_ref, h0re_ref, h0im_ref, are_ref, aim_ref, wbre_ref, wbim_ref,
               wcre_ref, wcim_ref, d_ref, wglu_ref, bglu_ref,
               ga_ref, sre_ref, sim_ref, bre_s, bim_s, *, ts):
    s = pl.program_id(1)

    @pl.when(s == 0)
    def _():
        sre_ref[...] = h0re_ref[...]
        sim_ref[...] = h0im_ref[...]

    nhalf = W_A // MXU_DIM
    hw = W_S // nhalf
    ncb = W_S // S5_COLS
    nsub = 2
    th = ts // nsub
    rsub = th * SUBLANES

    def rows(sb):
        return slice(sb * rsub, (sb + 1) * rsub)

    def b_proj(sb, i):
        hf, dst, w = i // 2, (bre_s, bim_s)[i % 2], (wbre_ref, wbim_ref)[i % 2]
        lhs = ua_ref[rows(sb), hf * MXU_DIM:(hf + 1) * MXU_DIM].astype(BF16)
        dst[rows(sb), hf * hw:(hf + 1) * hw] = _dot(lhs, w[hf])

    def scan(sb, cb):
        cols = slice(cb * S5_COLS, (cb + 1) * S5_COLS)
        ar = jnp.broadcast_to(are_ref[:, cols], (SUBLANES, S5_COLS))
        ai = jnp.broadcast_to(aim_ref[:, cols], (SUBLANES, S5_COLS))
        hr, hi = sre_ref[:, cols], sim_ref[:, cols]
        for t in range(th):
            r = slice(sb * rsub + t * SUBLANES, sb * rsub + (t + 1) * SUBLANES)
            hr, hi = ar * hr - ai * hi + bre_s[r, cols], ar * hi + ai * hr + bim_s[r, cols]
            bre_s[r, cols] = hr
            bim_s[r, cols] = hi
        sre_ref[:, cols] = hr
        sim_ref[:, cols] = hi

    def c_proj(sb, hf):
        hre = bre_s[rows(sb), hf * hw:(hf + 1) * hw].astype(BF16)
        him = bim_s[rows(sb), hf * hw:(hf + 1) * hw].astype(BF16)
        return _dot(hre, wcre_ref[hf]) + _dot(him, wcim_ref[hf])

    def finish(sb, ys):
        y = jnp.concatenate(ys, axis=-1) + d_ref[...] * ua_ref[rows(sb), :]
        y = jax.nn.gelu(y, approximate=True)
        y = y * _sigmoid(_dot(y.astype(BF16), wglu_ref[...]) + bglu_ref[...])
        ga_ref[rows(sb), :] = (y * _silu(za_ref[rows(sb), :])).astype(BF16)

    assert 2 * nhalf == ncb and nhalf == 2
    for i in range(2 * nhalf):
        b_proj(0, i)
    for i in range(ncb):
        b_proj(1, i)
        scan(0, i)
    ys = []
    for hf in range(nhalf):
        ys.append(c_proj(0, hf))
        scan(1, 2 * hf)
    finish(0, ys)
    scan(1, 1)
    scan(1, 3)
    finish(1, [c_proj(1, hf) for hf in range(nhalf)])


def s5_mixer(u3, h0re, h0im, w, ts):
    nbg, rows, _ = u3.shape
    rs = ts * SUBLANES
    nsteps = rows // rs
    st_spec = pl.BlockSpec((None, SUBLANES, W_S), lambda g, s: (g, 0, 0))
    nhalf = W_A // MXU_DIM
    hw = W_S // nhalf
    return pl.pallas_call(
        functools.partial(_s5_kernel, ts=ts),
        grid=(nbg, nsteps),
        in_specs=[pl.BlockSpec((None, rs, W_A), lambda g, s: (g, s, 0)),
                  pl.BlockSpec((None, rs, W_A), lambda g, s: (g, s, 1)),
                  st_spec, st_spec,
                  _const_spec((1, W_S)), _const_spec((1, W_S)),
                  _const_spec((nhalf, MXU_DIM, hw)), _const_spec((nhalf, MXU_DIM, hw)),
                  _const_spec((nhalf, hw, MXU_DIM)), _const_spec((nhalf, hw, MXU_DIM)),
                  _const_spec((1, W_A)), _const_spec((W_A, W_A)), _const_spec((1, W_A))],
        out_specs=[pl.BlockSpec((None, rs, W_A), lambda g, s: (g, s, 0)), st_spec, st_spec],
        out_shape=[jax.ShapeDtypeStruct((nbg, rows, W_A), BF16),
                   jax.ShapeDtypeStruct((nbg, SUBLANES, W_S), F32),
                   jax.ShapeDtypeStruct((nbg, SUBLANES, W_S), F32)],
        scratch_shapes=[pltpu.VMEM((rs, W_S), F32), pltpu.VMEM((rs, W_S), F32)],
        compiler_params=_cparams(("parallel", "arbitrary")),
        name="s5_mixer",
    )(u3, u3, h0re, h0im, w["s5_are"], w["s5_aim"], w["s5_wbre"], w["s5_wbim"],
      w["s5_wcre"], w["s5_wcim"], w["s5_d"], w["s5_wglu"], w["s5_bglu"])


def _conv_history(c, buf0_ref, xpad, buf1_ref, rc):
    @pl.when(c == 0)
    def _():
        xpad[0:HIST, :] = buf0_ref[...]
        buf1_ref[...] = buf0_ref[...]

    @pl.when(c > 0)
    def _():
        xpad[0:HIST, :] = xpad[rc:rc + HIST, :]


def _conv_step(live, x_ref, xpad, cw_ref, cb_ref, buf1_ref, rc):
    xpad[HIST:HIST + rc, :] = x_ref[...]
    buf1_ref[...] = jnp.where(live, xpad[rc:rc + HIST, :], buf1_ref[...])
    out = cb_ref[...]
    for tap in range(CONV_K):
        out = out + xpad[tap * SUBLANES:tap * SUBLANES + rc, :] * cw_ref[tap:tap + 1, :]
    return out


def _mlstm_kernel(x_ref, ga_ref, xb_ref, zb_ref, buf0_ref, c0_hbm, n0_ref, m0_ref, cw_ref, cb_ref,
                  wq_ref, wk_ref, wv_ref, wif_ref, bif_ref, nw_ref, skip_ref, wa_ref, wb_ref,
                  h1_ref, c1_hbm, n1_ref, m1_ref, buf1_ref,
                  xpad, xc_s, q_s, k_s, v_s, g_s, h_s, c_s, gb_s, xs, sem, *, tc, n_given):
    g = pl.program_id(0)
    c = pl.program_id(1)
    nch = pl.num_programs(1) - 1
    live = c < nch
    rc = tc * SUBLANES

    def state_copy(h, to_hbm):
        hbm = (c1_hbm if to_hbm else c0_hbm).at[pl.ds(g * SUBLANES, SUBLANES), h]
        src, dst = (c_s.at[h], hbm) if to_hbm else (hbm, c_s.at[h])
        return pltpu.make_async_copy(src, dst, sem.at[h])

    @pl.when(c == 0)
    def _():
        n1_ref[...] = n0_ref[...]
        m1_ref[...] = m0_ref[...]
        gb_s[...] = jnp.zeros(gb_s.shape, BF16)

    @pl.when(c == nch)
    def _():
        for h in range(H_B):
            state_copy(h, True).start()
        for h in range(H_B):
            state_copy(h, True).wait()

    @pl.when((c == 0) & (g < n_given))
    def _():
        for h in range(H_B):
            state_copy(h, False).start()
        for h in range(H_B):
            state_copy(h, False).wait()

    @pl.when((c == 0) & (g >= n_given))
    def _():
        c_s[...] = jnp.zeros(c_s.shape, F32)

    _conv_history(c, buf0_ref, xpad, buf1_ref, rc)

    for b in range(SUBLANES):
        _put_batch(xs, b, tc, 0, x_ref[b])

    xc = _silu(_conv_step(live, xb_ref, xpad, cw_ref, cb_ref, buf1_ref, rc))
    xc_s[...] = xc
    xc_bf = xc.astype(BF16)
    xb_bf = xb_ref[...].astype(BF16)
    gates = bif_ref[...]
    for h in range(H_B):
        cols = slice(h * DH_B, (h + 1) * DH_B)
        for i, (src, w_ref, dst) in enumerate(((xc_bf, wq_ref, q_s), (xc_bf, wk_ref, k_s), (xb_bf, wv_ref, v_s))):
            r = _dot(src[:, cols], w_ref[h])
            _put_cols(dst, h * DH_B, r)
            gates = gates + _dot(r.astype(BF16), wif_ref[i * W_B + h * DH_B:i * W_B + (h + 1) * DH_B, :])
    lane = lax.broadcasted_iota(jnp.int32, (rc, LANES), 1)
    gates = jnp.where(lane < H_B, gates, -_softplus(-gates))
    is_f = lax.broadcasted_iota(jnp.int32, (SUBLANES, LANES), 1) >= H_B
    run = jnp.zeros((SUBLANES, LANES), F32)
    for t in range(tc):
        cur = gates[t * SUBLANES:(t + 1) * SUBLANES, :]
        run = jnp.where(is_f, run + cur, cur)
        g_s[0, t * SUBLANES:(t + 1) * SUBLANES, :] = run

    ri = lax.broadcasted_iota(jnp.int32, (tc, tc), 0)
    ci = lax.broadcasted_iota(jnp.int32, (tc, tc), 1)
    causal = ri >= ci
    eye = ri == ci
    m_lane = lax.broadcasted_iota(jnp.int32, (SUBLANES, 1, LANES), 2)
    kscale = DH_B ** -0.5

    gt = _get_batches(g_s, tc, 0, LANES)
    m_old = m1_ref[...]
    m_new = m_old
    for h in range(H_B):
        cols = slice(h * DH_B, (h + 1) * DH_B)
        y_a = _dot(ga_ref[...], wa_ref[:, cols])

        ig = gt[:, :, h:h + 1]
        bc = gt[:, :, H_B + h:H_B + h + 1]
        a_row = jnp.sum(jnp.where(eye, ig - bc, 0.0), axis=1, keepdims=True)
        dlog = jnp.where(causal, bc + a_row, -jnp.inf)
        inter = bc + m_old[:, :, h:h + 1]
        m = jnp.maximum(inter, jnp.max(dlog, axis=2, keepdims=True))
        p = jnp.exp(dlog - m)
        q = _get_batches(q_s, tc, h * DH_B, DH_B)
        k = _get_batches(k_s, tc, h * DH_B, DH_B) * kscale
        v_bf = _get_batches(v_s, tc, h * DH_B, DH_B).astype(BF16)
        q_bf = q.astype(BF16)
        sc = _bdot_nt(q_bf, k.astype(BF16)) * p
        w_inter = jnp.exp(inter - m)
        c0 = c_s[h]
        n0 = n1_ref[:, :, cols]
        num = _bdot(sc.astype(BF16), v_bf) + w_inter * _bdot(q_bf, c0.astype(BF16))
        den = jnp.sum(sc, axis=2, keepdims=True) + w_inter * jnp.sum(q * n0, axis=2, keepdims=True)
        hv = num * (1.0 / jnp.maximum(jnp.abs(den), jnp.exp(-m)))
        y_b = _dot(gb_s[...], wb_ref[:, cols])
        _put_batches(h_s, tc, h * DH_B, _head_norm(hv))
        m_end = m[:, tc - 1:tc, :]
        w_end = jnp.exp(bc[:, tc - 1:tc, :] - bc + ig - m_end)
        decay = jnp.exp(inter[:, tc - 1:tc, :] - m_end)
        kw = k * w_end
        c_s[h] = decay * c0 + _bdot_tn(kw.astype(BF16), v_bf)
        n1_ref[:, :, cols] = jnp.where(live, decay * n0 + jnp.sum(kw, axis=1, keepdims=True), n0)
        m_new = jnp.where(m_lane == h, m_end, m_new)
        res = jnp.concatenate([xs[j] for j in range(h * DH_B // LANES, (h + 1) * DH_B // LANES)], axis=-1)
        h1_ref[:, cols] = res + y_a + y_b
    m1_ref[...] = jnp.where(live, m_new, m_old)

    y = _get_all(h_s) * nw_ref[...] + skip_ref[...] * xc_s[...]
    gb_s[...] = (y * _silu(zb_ref[...])).astype(BF16)


def mlstm_mixer(x, ga, u3, buf0, c0, n0, m0, w, tc):
    nbg, rows, _ = u3.shape
    rc = tc * SUBLANES
    nch = rows // rc
    grp = lambda shape: pl.BlockSpec((None,) + shape, lambda g, c: (g, 0, 0))
    per_b = lambda width: pl.BlockSpec((SUBLANES, 1, width), lambda g, c: (g, 0, 0))
    cur = lambda blk: (lambda g, c: (g, jnp.minimum(c, nch - 1), blk))
    prev = lambda g, c: (g, jnp.maximum(c - 1, 0), 0)
    blocked = pltpu.VMEM((W_B // LANES, rc, LANES), F32)
    n_given = c0.shape[0] // SUBLANES
    return pl.pallas_call(
        functools.partial(_mlstm_kernel, tc=tc, n_given=n_given),
        grid=(nbg, nch + 1),
        in_specs=[pl.BlockSpec((SUBLANES, tc, D_MODEL), prev),
                  pl.BlockSpec((None, rc, W_A), prev),
                  pl.BlockSpec((None, rc, W_B), cur(1)),
                  pl.BlockSpec((None, rc, W_B), cur(2)),
                  grp((HIST, W_B)),
                  pl.BlockSpec(memory_space=pl.ANY),
                  per_b(W_B), per_b(LANES),
                  _const_spec((CONV_K, W_B)), _const_spec((1, W_B)),
                  _const_spec((H_B, DH_B, DH_B)), _const_spec((H_B, DH_B, DH_B)), _const_spec((H_B, DH_B, DH_B)),
                  _const_spec((3 * W_B, LANES)), _const_spec((1, LANES)),
                  _const_spec((1, W_B)), _const_spec((1, W_B)),
                  _const_spec((W_A, D_MODEL)), _const_spec((W_B, D_MODEL))],
        out_specs=[pl.BlockSpec((None, rc, D_MODEL), prev),
                   pl.BlockSpec(memory_space=pl.ANY),
                   per_b(W_B), per_b(LANES), grp((HIST, W_B))],
        out_shape=[jax.ShapeDtypeStruct((nbg, rows, D_MODEL), F32),
                   jax.ShapeDtypeStruct((nbg * SUBLANES,) + c0.shape[1:], F32),
                   jax.ShapeDtypeStruct((nbg * SUBLANES, 1, W_B), F32),
                   jax.ShapeDtypeStruct((nbg * SUBLANES, 1, LANES), F32),
                   jax.ShapeDtypeStruct((nbg, HIST, W_B), F32)],
        scratch_shapes=[pltpu.VMEM((rc + HIST, W_B), F32),
                        pltpu.VMEM((rc, W_B), F32), blocked, blocked, blocked,
                        pltpu.VMEM((1, rc, LANES), F32), blocked,
                        pltpu.VMEM((H_B, SUBLANES, DH_B, DH_B), F32),
                        pltpu.VMEM((rc, W_B), BF16), blocked,
                        pltpu.SemaphoreType.DMA((H_B,))],
        compiler_params=_cparams(("parallel", "arbitrary")),
        name="mlstm_mixer",
    )(x, ga, u3, u3, buf0, c0, n0, m0, w["ml_cw"], w["ml_cb"], w["ml_wq"], w["ml_wk"], w["ml_wv"],
      w["ml_wif"], w["ml_bif"], w["ml_nw"], w["ml_skip"], w["ev_w_out_a"], w["ev_w_out_b"])


def _ret_log_decay(h):
    return float(np.log1p(-np.exp2(-5.0 - h)))


OD_MIX_IN = 2 * QK_C + 2 * W_C


def _retention_kernel(h_ref, g1_ref, w_ref, cos_ref, sin_ref, s0_hbm, nw_ref,
                      gc_ref, s1_hbm, ud_ref, hn_s, slot, zc_s, q_s, k_s, v_s, o_s, st_s, sem, *, tc, n_given):
    g = pl.program_id(0)
    c = pl.program_id(1)
    started = c > 0

    @pl.when(c == 0)
    def _():
        slot[...] = jnp.zeros(slot.shape, F32)

    def state_copy(h, to_hbm):
        hbm = (s1_hbm if to_hbm else s0_hbm).at[pl.ds(g * SUBLANES, SUBLANES), h]
        src, dst = (st_s.at[h], hbm) if to_hbm else (hbm, st_s.at[h])
        return pltpu.make_async_copy(src, dst, sem.at[h])

    @pl.when((c == 0) & (g < n_given))
    def _():
        for h in range(H_C):
            state_copy(h, False).start()
        for h in range(H_C):
            state_copy(h, False).wait()

    @pl.when((c == 0) & (g >= n_given))
    def _():
        st_s[...] = jnp.zeros(st_s.shape, F32)

    hn_s[...] = _rms(h_ref[...], g1_ref[...]).astype(BF16)

    def project(j):
        cols = slice(j * MXU_DIM, (j + 1) * MXU_DIM)
        r = _dot(hn_s[...], w_ref[:, cols])
        if (j + 1) * MXU_DIM <= OD_MIX_IN:
            slot[:, cols] = r
        else:
            ud_ref[:, j * MXU_DIM - OD_MIX_IN:(j + 1) * MXU_DIM - OD_MIX_IN] = r

    lru_blocks = iter(range(OD_MIX_IN // MXU_DIM, OD_IN // MXU_DIM))
    per_copy = (OD_IN - OD_MIX_IN) // MXU_DIM // H_C
    cos, sin = cos_ref[...], sin_ref[...]
    _put_cols(v_s, 0, slot[:, 2 * QK_C:2 * QK_C + W_C])
    zc_s[...] = slot[:, 2 * QK_C + W_C:OD_MIX_IN]
    for h in range(H_C):
        for _ in range(per_copy):
            project(next(lru_blocks))
        for off, dst, scale in ((0, q_s, None), (QK_C, k_s, DK_C ** -0.5)):
            x = slot[:, off + h * DK_C:off + (h + 1) * DK_C]
            r = x * cos + pltpu.roll(x, DK_C // 2, 1) * sin
            dst[h] = r if scale is None else r * scale

    per_head = OD_MIX_IN // MXU_DIM // H_C

    ri = lax.broadcasted_iota(jnp.int32, (tc, tc), 0)
    ci = lax.broadcasted_iota(jnp.int32, (tc, tc), 1)
    diff = (ri - ci).astype(F32)
    tcol = lax.broadcasted_iota(jnp.int32, (tc, 1), 0).astype(F32)
    dmask, xi, zeta, gfull = [], [], [], []
    for h in range(H_C):
        lg = _ret_log_decay(h)
        dmask.append(jnp.where(diff >= 0, jnp.exp(lg * jnp.maximum(diff, 0.0)), 0.0))
        xi.append(jnp.exp(lg * (tcol + 1.0)))
        zeta.append(jnp.exp(lg * (tc - 1.0 - tcol)))
        gfull.append(float(np.exp(np.float32(lg) * np.float32(tc))))

    for h in range(H_C):
        nxt = iter(range(h * per_head, (h + 1) * per_head))
        q_bf = _get_batches(q_s, tc, h * DK_C, DK_C).astype(BF16)
        k = _get_batches(k_s, tc, h * DK_C, DK_C)
        v_bf = _get_batches(v_s, tc, h * DV_C, DV_C).astype(BF16)
        s0 = st_s[h]
        inner = _bdot_nt(q_bf, k.astype(BF16)) * dmask[h]
        project(next(nxt))
        o = _bdot(inner.astype(BF16), v_bf) + xi[h] * _bdot(q_bf, s0.astype(BF16))
        project(next(nxt))
        _put_batches(o_s, tc, h * DV_C, _head_norm(o))
        decay = jnp.where(started, gfull[h], 1.0)
        st_s[h] = decay * s0 + _bdot_tn((k * zeta[h]).astype(BF16), v_bf)
        for j in nxt:
            project(j)

    gc_ref[...] = (_get_all(o_s) * nw_ref[...] * _silu(zc_s[...])).astype(BF16)

    @pl.when(c == pl.num_programs(1) - 1)
    def _():
        for h in range(H_C):
            state_copy(h, True).start()
        for h in range(H_C):
            state_copy(h, True).wait()


def retention_mixer(h1, cos, sin, s0, w, tc):
    nbg, rows, d = h1.shape
    rc = tc * SUBLANES
    nch = rows // rc
    assert OD_MIX_IN % (MXU_DIM * H_C) == 0 and (OD_IN - OD_MIX_IN) % (MXU_DIM * H_C) == 0
    n_given = s0.shape[0] // SUBLANES
    cur = lambda g, c: (g, jnp.minimum(c, nch - 1), 0)
    prev = lambda g, c: (g, jnp.maximum(c - 1, 0), 0)
    return pl.pallas_call(
        functools.partial(_retention_kernel, tc=tc, n_given=n_given),
        grid=(nbg, nch + 1),
        in_specs=[pl.BlockSpec((None, rc, d), cur),
                  _const_spec((1, d)), _const_spec((d, OD_IN)),
                  pl.BlockSpec((None, rc, DK_C), prev),
                  pl.BlockSpec((None, rc, DK_C), prev),
                  pl.BlockSpec(memory_space=pl.ANY),
                  _const_spec((1, W_C))],
        out_specs=[pl.BlockSpec((None, rc, W_C), prev),
                   pl.BlockSpec(memory_space=pl.ANY),
                   pl.BlockSpec((None, rc, OD_IN - OD_MIX_IN), cur)],
        out_shape=[jax.ShapeDtypeStruct((nbg, rows, W_C), BF16),
                   jax.ShapeDtypeStruct((nbg * SUBLANES,) + s0.shape[1:], F32),
                   jax.ShapeDtypeStruct((nbg, rows, OD_IN - OD_MIX_IN), F32)],
        scratch_shapes=[pltpu.VMEM((rc, d), BF16), pltpu.VMEM((rc, OD_MIX_IN), F32), pltpu.VMEM((rc, W_C), F32),
                        pltpu.VMEM((QK_C // LANES, rc, LANES), F32), pltpu.VMEM((QK_C // LANES, rc, LANES), F32),
                        pltpu.VMEM((W_C // LANES, rc, LANES), F32), pltpu.VMEM((W_C // LANES, rc, LANES), F32),
                        pltpu.VMEM((H_C, SUBLANES, DK_C, DV_C), F32),
                        pltpu.SemaphoreType.DMA((H_C,))],
        compiler_params=_cparams(("parallel", "arbitrary")),
        name="retention_mixer",
    )(h1, w["norm1"], w["od_w_in"], cos, sin, s0, w["ret_nw"])


def _rglru_kernel(res_ref, gc_ref, xd_ref, zd_ref, buf0_ref, h0_ref, cw_ref, cb_ref, wa_ref, ba_ref, wx_ref,
                  bx_ref, lam_ref, woc_ref, wod_ref, fn_ref,
                  y_ref, h1_ref, buf1_ref, xpad, a_s, b_s, gd_s, ys, inv_s, *, tc):
    c = pl.program_id(1)
    nch = pl.num_programs(1) - 1
    live = c < nch
    rc = tc * SUBLANES

    @pl.when(c == 0)
    def _():
        h1_ref[...] = h0_ref[...]
        gd_s[...] = jnp.zeros(gd_s.shape, BF16)

    _conv_history(c, buf0_ref, xpad, buf1_ref, rc)

    xpad[HIST:HIST + rc, :] = xd_ref[...]
    buf1_ref[...] = jnp.where(live, xpad[rc:rc + HIST, :], buf1_ref[...])
    sp = _softplus(-lam_ref[...])
    ssq = jnp.zeros((rc, 1), F32)
    for j in range(W_D // MXU_DIM):
        cols = slice(j * MXU_DIM, (j + 1) * MXU_DIM)
        y_c = _dot(gc_ref[...], woc_ref[:, cols])
        xc = cb_ref[:, cols]
        for tap in range(CONV_K):
            xc = xc + xpad[tap * SUBLANES:tap * SUBLANES + rc, cols] * cw_ref[tap:tap + 1, cols]
        lhs = xc.astype(BF16)
        r = _sigmoid(_dot(lhs, wa_ref[j]) + ba_ref[:, cols])
        y_d = _dot(gd_s[...], wod_ref[:, cols])
        i = _sigmoid(_dot(lhs, wx_ref[j]) + bx_ref[:, cols])
        log_a = -LRU_C * r * sp[:, cols]
        a = jnp.exp(log_a)
        a_s[:, cols] = a
        b_s[:, cols] = jnp.sqrt(-jnp.tanh(log_a) * (a * a + 1.0)) * (i * xc)
        yj = res_ref[:, cols] + y_c + y_d
        _put_cols(ys, j * MXU_DIM, yj)
        ssq = ssq + jnp.sum(yj * yj, axis=-1, keepdims=True)

    inv_s[0] = jnp.broadcast_to(lax.rsqrt(ssq * (1.0 / D_MODEL) + EPS), (rc, LANES))
    for b in range(SUBLANES):
        inv_b = _get_batch(inv_s, b, tc, 0, LANES)[:, 0:1]
        y_ref[b] = _get_batch(ys, b, tc, 0, D_MODEL) * inv_b * fn_ref[...]

    def step(t, h):
        r0 = pl.multiple_of(t * SUBLANES, SUBLANES)
        h = a_s[pl.ds(r0, SUBLANES), :] * h + b_s[pl.ds(r0, SUBLANES), :]
        b_s[pl.ds(r0, SUBLANES), :] = h
        return h

    h_old = h1_ref[...]
    h1_ref[...] = jnp.where(live, lax.fori_loop(0, tc, step, h_old, unroll=4), h_old)
    gd_s[...] = (b_s[...] * _silu(zd_ref[...])).astype(BF16)


def rglru_mixer(res, gc, u3, buf0, h0, w, tc):
    nbg, rows, _ = u3.shape
    rc = tc * SUBLANES
    nch = rows // rc
    nblk = W_D // MXU_DIM
    grp = lambda shape: pl.BlockSpec((None,) + shape, lambda g, c: (g, 0, 0))
    cur = lambda blk: (lambda g, c: (g, jnp.minimum(c, nch - 1), blk))
    prev = lambda g, c: (g, jnp.maximum(c - 1, 0), 0)
    return pl.pallas_call(
        functools.partial(_rglru_kernel, tc=tc),
        grid=(nbg, nch + 1),
        in_specs=[pl.BlockSpec((None, rc, D_MODEL), prev),
                  pl.BlockSpec((None, rc, W_C), prev),
                  pl.BlockSpec((None, rc, W_D), cur(0)),
                  pl.BlockSpec((None, rc, W_D), cur(1)),
                  grp((HIST, W_D)), grp((SUBLANES, W_D)),
                  _const_spec((CONV_K, W_D)), _const_spec((1, W_D)),
                  _const_spec((nblk, MXU_DIM, MXU_DIM)), _const_spec((1, W_D)),
                  _const_spec((nblk, MXU_DIM, MXU_DIM)), _const_spec((1, W_D)),
                  _const_spec((1, W_D)),
                  _const_spec((W_C, D_MODEL)), _const_spec((W_D, D_MODEL)), _const_spec((1, D_MODEL))],
        out_specs=[pl.BlockSpec((SUBLANES, tc, D_MODEL), prev),
                   grp((SUBLANES, W_D)), grp((HIST, W_D))],
        out_shape=[jax.ShapeDtypeStruct((nbg * SUBLANES, rows // SUBLANES, D_MODEL), F32),
                   jax.ShapeDtypeStruct((nbg, SUBLANES, W_D), F32),
                   jax.ShapeDtypeStruct((nbg, HIST, W_D), F32)],
        scratch_shapes=[pltpu.VMEM((rc + HIST, W_D), F32), pltpu.VMEM((rc, W_D), F32), pltpu.VMEM((rc, W_D), F32),
                        pltpu.VMEM((rc, W_D), BF16), pltpu.VMEM((D_MODEL // LANES, rc, LANES), F32),
                        pltpu.VMEM((1, rc, LANES), F32)],
        compiler_params=_cparams(("parallel", "arbitrary")),
        name="rglru_mixer",
    )(res, gc, u3, u3, buf0, h0, w["lru_cw"], w["lru_cb"], w["lru_wa"], w["lru_ba"], w["lru_wx"], w["lru_bx"],
      w["lru_lam"], w["od_w_out_c"], w["od_w_out_d"], w["final_norm"])


def _to_tm(x):
    b, t, c = x.shape
    return x.reshape(b // SUBLANES, SUBLANES, t, c).swapaxes(1, 2).reshape(b // SUBLANES, t * SUBLANES, c)


def _from_tm(x, t):
    nbg, _, c = x.shape
    return x.reshape(nbg, t, SUBLANES, c).swapaxes(1, 2).reshape(nbg * SUBLANES, t, c)


def _rope_tables(pos):
    half = DK_C // 2
    inv = ROPE_BASE ** (-jnp.arange(half, dtype=F32) / half)
    ang = pos[..., None] * inv
    cos = jnp.cos(ang)
    sin = jnp.sin(ang)
    cos = jnp.concatenate([cos, cos], axis=-1)
    sin = jnp.concatenate([-sin, sin], axis=-1)
    rep = lambda a: jnp.repeat(a, SUBLANES, axis=1)
    return rep(cos), rep(sin)


def _prep_weights(p):
    w = {}
    w["norm0"] = p["norm_w"][0][None]
    w["norm1"] = p["norm_w"][1][None]
    w["final_norm"] = p["final_norm_w"][None]
    w["ev_w_in"] = p["ev_w_in"][0].astype(BF16)
    w["ev_w_out_a"] = p["ev_w_out"][0][:W_A].astype(BF16)
    w["ev_w_out_b"] = p["ev_w_out"][0][W_A:].astype(BF16)
    are, aim, bbre, bbim = s5_prep(p["s5_lambda_re"][0], p["s5_lambda_im"][0], p["s5_log_dt"][0],
                                   p["s5_b_re"][0], p["s5_b_im"][0])
    per = MXU_DIM // S5_GROUP
    w["s5_are"] = are.reshape(1, W_S)
    w["s5_aim"] = aim.reshape(1, W_S)
    w["s5_wbre"] = _block_diag(bbre, per).astype(BF16)
    w["s5_wbim"] = _block_diag(bbim, per).astype(BF16)
    w["s5_wcre"] = _block_diag(jnp.swapaxes(p["s5_c_re"][0], 1, 2), per).astype(BF16)
    w["s5_wcim"] = _block_diag(-jnp.swapaxes(p["s5_c_im"][0], 1, 2), per).astype(BF16)
    w["s5_d"] = p["s5_d"][0][None]
    w["s5_wglu"] = p["s5_w_glu"][0].astype(BF16)
    w["s5_bglu"] = p["s5_b_glu"][0][None]
    w["ml_cw"] = p["ml_conv_w"][0]
    w["ml_cb"] = p["ml_conv_b"][0][None]
    w["ml_wq"] = p["ml_wq"][0].astype(BF16)
    w["ml_wk"] = p["ml_wk"][0].astype(BF16)
    w["ml_wv"] = p["ml_wv"][0].astype(BF16)
    w["ml_wif"] = jnp.pad(p["ml_w_if"][0], ((0, 0), (0, LANES - 2 * H_B))).astype(BF16)
    w["ml_bif"] = jnp.pad(p["ml_b_if"][0], (0, LANES - 2 * H_B))[None]
    w["ml_nw"] = p["ml_norm_w"][0][None]
    w["ml_skip"] = p["ml_skip"][0][None]
    w["od_w_in"] = p["od_w_in"][0].astype(BF16)
    w["od_w_out_c"] = p["od_w_out"][0][:W_C].astype(BF16)
    w["od_w_out_d"] = p["od_w_out"][0][W_C:].astype(BF16)
    w["ret_nw"] = p["ret_norm_w"][0][None]
    w["lru_cw"] = p["lru_conv_w"][0]
    w["lru_cb"] = p["lru_conv_b"][0][None]
    w["lru_wa"] = _block_diag(p["lru_w_a"][0], MXU_DIM // BD_D).astype(BF16)
    w["lru_ba"] = p["lru_b_a"][0][None]
    w["lru_wx"] = _block_diag(p["lru_w_x"][0], MXU_DIM // BD_D).astype(BF16)
    w["lru_bx"] = p["lru_b_x"][0][None]
    w["lru_lam"] = p["lru_lambda"][0][None]
    return w


def _conv_to_tm(buf):
    return _to_tm(buf)


def _trunk(x, pos, st, w, tc):
    new = {}
    u = proj_in(x, w["norm0"], w["ev_w_in"], True)
    ga, new["s5_re"], new["s5_im"] = s5_mixer(u, st["s5_re"], st["s5_im"], w, tc)
    h1, new["ml_c"], new["ml_n"], new["ml_m"], new["ml_conv"] = mlstm_mixer(
        x, ga, u, st["ml_conv"], st["ml_c"], st["ml_n"], st["ml_m"], w, tc)
    cos, sin = _rope_tables(pos)
    gc, new["ret"], ud = retention_mixer(h1, cos, sin, st["ret"], w, tc)
    y, new["lru_h"], new["lru_conv"] = rglru_mixer(h1, gc, ud, st["lru_conv"], st["lru_h"], w, tc)
    return y, new


def _states_in(s5_re, s5_im, ml_c, ml_n, ml_m, ml_conv, ret, lru_h, lru_conv):
    b = s5_re.shape[0]
    nbg = b // SUBLANES
    return dict(
        s5_re=s5_re.reshape(nbg, SUBLANES, W_S),
        s5_im=s5_im.reshape(nbg, SUBLANES, W_S),
        ml_c=ml_c,
        ml_n=ml_n.reshape(b, 1, W_B),
        ml_m=jnp.pad(ml_m, ((0, 0), (0, LANES - H_B))).reshape(b, 1, LANES),
        ml_conv=_conv_to_tm(ml_conv),
        ret=ret,
        lru_h=lru_h.reshape(nbg, SUBLANES, W_D),
        lru_conv=_conv_to_tm(lru_conv),
    )


def _states_out(st, sel=None):
    nbg = st["s5_re"].shape[0]
    b = nbg * SUBLANES
    out = dict(
        s5_re=st["s5_re"].reshape(b, G_A, S5_STATE),
        s5_im=st["s5_im"].reshape(b, G_A, S5_STATE),
        ml_c=st["ml_c"].reshape(b, H_B, DH_B, DH_B),
        ml_n=st["ml_n"].reshape(b, H_B, DH_B),
        ml_m=st["ml_m"].reshape(b, LANES)[:, :H_B],
        ml_conv=_from_tm(st["ml_conv"], CONV_K - 1),
        ret=st["ret"].reshape(b, H_C, DK_C, DV_C),
        lru_h=st["lru_h"].reshape(b, W_D),
        lru_conv=_from_tm(st["lru_conv"], CONV_K - 1),
    )
    if sel is not None:
        out = {k: v[sel] for k, v in out.items()}
    return out


_STATE_ORDER = ("s5_re", "s5_im", "ml_c", "ml_n", "ml_m", "ml_conv", "ret", "lru_h", "lru_conv")


def kernel(x_prompt, x_sample, state_s5_re, state_s5_im, state_ml_c, state_ml_n, state_ml_m, state_ml_conv, state_ret, state_lru_h, state_lru_conv, meta, norm_w, final_norm_w, ev_w_in, ev_w_out, s5_lambda_re, s5_lambda_im, s5_log_dt, s5_b_re, s5_b_im, s5_c_re, s5_c_im, s5_d, s5_w_glu, s5_b_glu, ml_conv_w, ml_conv_b, ml_wq, ml_wk, ml_wv, ml_w_if, ml_b_if, ml_norm_w, ml_skip, od_w_in, od_w_out, ret_norm_w, lru_conv_w, lru_conv_b, lru_w_a, lru_b_a, lru_w_x, lru_b_x, lru_lambda):
    p = dict(norm_w=norm_w, final_norm_w=final_norm_w, ev_w_in=ev_w_in, ev_w_out=ev_w_out,
             s5_lambda_re=s5_lambda_re, s5_lambda_im=s5_lambda_im, s5_log_dt=s5_log_dt,
             s5_b_re=s5_b_re, s5_b_im=s5_b_im, s5_c_re=s5_c_re, s5_c_im=s5_c_im, s5_d=s5_d,
             s5_w_glu=s5_w_glu, s5_b_glu=s5_b_glu, ml_conv_w=ml_conv_w, ml_conv_b=ml_conv_b,
             ml_wq=ml_wq, ml_wk=ml_wk, ml_wv=ml_wv, ml_w_if=ml_w_if, ml_b_if=ml_b_if,
             ml_norm_w=ml_norm_w, ml_skip=ml_skip, od_w_in=od_w_in, od_w_out=od_w_out,
             ret_norm_w=ret_norm_w, lru_conv_w=lru_conv_w, lru_conv_b=lru_conv_b,
             lru_w_a=lru_w_a, lru_b_a=lru_b_a, lru_w_x=lru_w_x, lru_b_x=lru_b_x, lru_lambda=lru_lambda)
    assert ev_w_in.shape[0] == 1 and od_w_in.shape[0] == 1, "two-layer trunk"
    w = _prep_weights(p)
    bp, tp, _ = x_prompt.shape
    bs, ts, _ = x_sample.shape
    assert ts == N_META and bp % SUBLANES == 0 and bs % SUBLANES == 0 and tp % CHUNK == 0

    meta_b = jnp.broadcast_to(meta[None], (bp, N_META, D_MODEL))
    x_short = jnp.concatenate([x_sample, meta_b], axis=0)
    given = dict(s5_re=state_s5_re[0], s5_im=state_s5_im[0], ml_c=state_ml_c[0], ml_n=state_ml_n[0],
                 ml_m=state_ml_m[0], ml_conv=state_ml_conv[0], ret=state_ret[0], lru_h=state_lru_h[0],
                 lru_conv=state_lru_conv[0])
    both = {k: s if k in ("ml_c", "ret") else jnp.concatenate([s, jnp.zeros((bp,) + s.shape[1:], s.dtype)], axis=0)
            for k, s in given.items()}
    tpos = jnp.arange(N_META, dtype=F32)
    pos_short = jnp.concatenate([jnp.broadcast_to((N_META + PAST_LEN) + tpos, (bs // SUBLANES, N_META)),
                                 jnp.broadcast_to(tpos, (bp // SUBLANES, N_META))], axis=0)
    y_short, st_short = _trunk(x_short, pos_short, _states_in(**both), w, N_META)
    y_sample = y_short[:bs]
    new_s = _states_out(st_short, slice(0, bs))

    nbs = bs // SUBLANES
    nbg_short = (bs + bp) // SUBLANES
    st_meta = {}
    for name, val in st_short.items():
        per_group = val.shape[0] // nbg_short
        st_meta[name] = val[nbs * per_group:]
    pos_long = jnp.broadcast_to(N_META + jnp.arange(tp, dtype=F32), (bp // SUBLANES, tp))
    y_prompt, st_long = _trunk(x_prompt, pos_long, st_meta, w, CHUNK)
    new_p = _states_out(st_long)

    return ((y_prompt, y_sample)
            + tuple(new_p[k][None] for k in _STATE_ORDER)
            + tuple(new_s[k][None] for k in _STATE_ORDER))
```

```python
import functools
import math

import numpy as np
import jax
import jax.numpy as jnp
from jax import lax
from jax.experimental import pallas as pl
from jax.experimental.pallas import tpu as pltpu

F32 = jnp.float32
BF16 = jnp.bfloat16

SUBLANES = 8
LANES = 128
MXU_DIM = 256
VMEM_LIMIT = 60 * 1024 * 1024

D_MODEL = 1024
CHUNK = 64
N_META = 16
PAST_LEN = 2048
EPS = 1e-6
CONV_K = 4
S5_GROUP = 16
S5_STATE = 64
W_A = D_MODEL // 2
G_A = W_A // S5_GROUP
W_S = G_A * S5_STATE
W_B = D_MODEL
H_B = 4
DH_B = W_B // H_B
W_C = D_MODEL
H_C = 4
DV_C = W_C // H_C
DK_C = DV_C // 2
QK_C = H_C * DK_C
ROPE_BASE = 10000.0
W_D = D_MODEL
H_D = 8
BD_D = W_D // H_D
LRU_C = 8.0
EV_IN = 2 * W_A + 2 * W_B
OD_IN = 2 * QK_C + 2 * W_C + 2 * W_D
HIST = (CONV_K - 1) * SUBLANES

ROW_TILE = 1024
S5_COLS = 512


def _cparams(sem):
    return pltpu.CompilerParams(dimension_semantics=sem, vmem_limit_bytes=VMEM_LIMIT)


def _const_spec(shape):
    nd = len(shape)
    return pl.BlockSpec(shape, lambda *_: (0,) * nd, pipeline_mode=pl.Buffered(1))


def _dot(a, b):
    return jnp.dot(a, b, preferred_element_type=F32)


def _dot_nt(a, b):
    return lax.dot_general(a, b, (((1,), (1,)), ((), ())), preferred_element_type=F32)


def _dot_tn(a, b):
    return lax.dot_general(a, b, (((0,), (0,)), ((), ())), preferred_element_type=F32)


def _rms(x, g):
    return x * lax.rsqrt(jnp.mean(x * x, axis=-1, keepdims=True) + EPS) * g


def _softplus(x):
    return jnp.maximum(x, 0.0) + jnp.log1p(jnp.exp(-jnp.abs(x)))


def _sigmoid(x):
    return 0.5 * jnp.tanh(0.5 * x) + 0.5


def _silu(x):
    h = 0.5 * x
    return h * jnp.tanh(h) + h


def _put_cols(dst3, col0, val):
    for j in range(val.shape[1] // LANES):
        dst3[col0 // LANES + j] = val[:, j * LANES:(j + 1) * LANES]


def _get_all(src3):
    return jnp.concatenate([src3[j] for j in range(src3.shape[0])], axis=-1)


def _get_batch(src3, b, tc, col0, width):
    parts = [src3[col0 // LANES + j, pl.ds(b, tc, stride=SUBLANES), :] for j in range(width // LANES)]
    return parts[0] if len(parts) == 1 else jnp.concatenate(parts, axis=-1)


def _put_batch(dst3, b, tc, col0, val):
    for j in range(val.shape[1] // LANES):
        dst3[col0 // LANES + j, pl.ds(b, tc, stride=SUBLANES), :] = val[:, j * LANES:(j + 1) * LANES]


def _get_batches(src3, tc, col0, width):
    return jnp.stack([_get_batch(src3, b, tc, col0, width) for b in range(SUBLANES)], axis=0)


def _put_batches(dst3, tc, col0, val):
    for b in range(SUBLANES):
        _put_batch(dst3, b, tc, col0, val[b])


def _bdot(a, b):
    return lax.dot_general(a, b, (((2,), (1,)), ((0,), (0,))), preferred_element_type=F32)


def _bdot_nt(a, b):
    return lax.dot_general(a, b, (((2,), (2,)), ((0,), (0,))), preferred_element_type=F32)


def _bdot_tn(a, b):
    return lax.dot_general(a, b, (((1,), (1,)), ((0,), (0,))), preferred_element_type=F32)


def _head_norm(h):
    mu = jnp.mean(h, axis=-1, keepdims=True)
    d = h - mu
    var = jnp.mean(d * d, axis=-1, keepdims=True)
    return d * lax.rsqrt(var + EPS)


def _load_bm_as_tm(x_ref, xs):
    tt = x_ref.shape[1]
    for b in range(SUBLANES):
        _put_batch(xs, b, tt, 0, x_ref[b])
    return _get_all(xs)


def _act_spec(bm, tt, width):
    if bm:
        return pl.BlockSpec((SUBLANES, tt, width), lambda g, i: (g, i, 0))
    return pl.BlockSpec((None, tt * SUBLANES, width), lambda g, i: (g, i, 0))


def _act_dims(x, bm):
    return (x.shape[0] // SUBLANES, x.shape[1]) if bm else (x.shape[0], x.shape[1] // SUBLANES)


def _proj_in_kernel(x_ref, g_ref, w_ref, o_ref, *scratch, x_bm):
    x = _load_bm_as_tm(x_ref, scratch[0]) if x_bm else x_ref[...]
    o_ref[...] = _dot(_rms(x, g_ref[...]).astype(BF16), w_ref[...])


def proj_in(x, g, w, x_bm):
    d, n = w.shape
    nbg, t = _act_dims(x, x_bm)
    tt = min(ROW_TILE // SUBLANES, t)
    rows = tt * SUBLANES
    return pl.pallas_call(
        functools.partial(_proj_in_kernel, x_bm=x_bm),
        grid=(nbg, t // tt),
        in_specs=[_act_spec(x_bm, tt, d), _const_spec((1, d)), _const_spec((d, n))],
        out_specs=_act_spec(False, tt, n),
        out_shape=jax.ShapeDtypeStruct((nbg, t * SUBLANES, n), F32),
        scratch_shapes=[pltpu.VMEM((d // LANES, rows, LANES), F32)] if x_bm else [],
        compiler_params=_cparams(("parallel", "parallel")),
        name="proj_in",
    )(x, g, w)


def _s5_prep_kernel(lre_ref, lim_ref, ldt_ref, bre_ref, bim_ref, are_ref, aim_ref, bbre_ref, bbim_ref):
    lre, lim = lre_ref[...], lim_ref[...]
    dt = jnp.exp(ldt_ref[...])
    mag = jnp.exp(lre * dt)
    ab_re = mag * jnp.cos(lim * dt)
    ab_im = mag * jnp.sin(lim * dt)
    den = lre * lre + lim * lim
    nr = ab_re - 1.0
    k_re = (nr * lre + ab_im * lim) / den
    k_im = (ab_im * lre - nr * lim) / den
    bre, bim = bre_ref[...], bim_ref[...]
    are_ref[...] = ab_re
    aim_ref[...] = ab_im
    bbre_ref[...] = k_re * bre - k_im * bim
    bbim_ref[...] = k_re * bim + k_im * bre


def s5_prep(lam_re, lam_im, log_dt, b_re, b_im):
    rep = lambda a: jnp.repeat(a, S5_GROUP, axis=0)
    rows = G_A * S5_GROUP
    ldt = jnp.broadcast_to(rep(log_dt[:, None]), (rows, S5_STATE))
    tr = lambda b: jnp.swapaxes(b, 1, 2).reshape(rows, S5_STATE)
    outs = pl.pallas_call(
        _s5_prep_kernel,
        out_shape=[jax.ShapeDtypeStruct((rows, S5_STATE), F32)] * 4,
        name="s5_prep",
    )(rep(lam_re), rep(lam_im), ldt, tr(b_re), tr(b_im))
    are, aim, bbre, bbim = (o.reshape(G_A, S5_GROUP, S5_STATE) for o in outs)
    return are[:, 0], aim[:, 0], bbre, bbim


def _block_diag(blocks, per):
    n, r, c = blocks.shape
    b = blocks.reshape(n // per, per, r, c)
    eye = jnp.eye(per, dtype=blocks.dtype)
    return jnp.einsum("hgrc,gk->hgrkc", b, eye).reshape(n // per, per * r, per * c)


def _s5_kernel(ua_ref, za_ref, h0re_ref, h0im_ref, are_ref, aim_ref, wbre_ref, wbim_ref,
               wcre_ref, wcim_ref, d_ref, wglu_ref, bglu_ref,
               ga_ref, sre_ref, sim_ref, bre_s, bim_s, *, ts):
    s = pl.program_id(1)

    @pl.when(s == 0)
    def _():
        sre_ref[...] = h0re_ref[...]
        sim_ref[...] = h0im_ref[...]

    nhalf = W_A // MXU_DIM
    hw = W_S // nhalf
    ncb = W_S // S5_COLS
    nsub = 2
    th = ts // nsub
    rsub = th * SUBLANES

    def rows(sb):
        return slice(sb * rsub, (sb + 1) * rsub)

    def b_proj(sb, i):
        hf, dst, w = i // 2, (bre_s, bim_s)[i % 2], (wbre_ref, wbim_ref)[i % 2]
        lhs = ua_ref[rows(sb), hf * MXU_DIM:(hf + 1) * MXU_DIM].astype(BF16)
        dst[rows(sb), hf * hw:(hf + 1) * hw] = _dot(lhs, w[hf])

    def scan(sb, cb):
        cols = slice(cb * S5_COLS, (cb + 1) * S5_COLS)
        ar = jnp.broadcast_to(are_ref[:, cols], (SUBLANES, S5_COLS))
        ai = jnp.broadcast_to(aim_ref[:, cols], (SUBLANES, S5_COLS))
        hr, hi = sre_ref[:, cols], sim_ref[:, cols]
        for t in range(th):
            r = slice(sb * rsub + t * SUBLANES, sb * rsub + (t + 1) * SUBLANES)
            hr, hi = ar * hr - ai * hi + bre_s[r, cols], ar * hi + ai * hr + bim_s[r, cols]
            bre_s[r, cols] = hr
            bim_s[r, cols] = hi
        sre_ref[:, cols] = hr
        sim_ref[:, cols] = hi

    def c_proj(sb, hf):
        hre = bre_s[rows(sb), hf * hw:(hf + 1) * hw].astype(BF16)
        him = bim_s[rows(sb), hf * hw:(hf + 1) * hw].astype(BF16)
        return _dot(hre, wcre_ref[hf]) + _dot(him, wcim_ref[hf])

    def finish(sb, ys):
        y = jnp.concatenate(ys, axis=-1) + d_ref[...] * ua_ref[rows(sb), :]
        y = jax.nn.gelu(y, approximate=True)
        y = y * _sigmoid(_dot(y.astype(BF16), wglu_ref[...]) + bglu_ref[...])
        ga_ref[rows(sb), :] = (y * _silu(za_ref[rows(sb), :])).astype(BF16)

    assert 2 * nhalf == ncb and nhalf == 2
    for i in range(2 * nhalf):
        b_proj(0, i)
    for i in range(ncb):
        b_proj(1, i)
        scan(0, i)
    ys = []
    for hf in range(nhalf):
        ys.append(c_proj(0, hf))
        scan(1, 2 * hf)
    finish(0, ys)
    scan(1, 1)
    scan(1, 3)
    finish(1, [c_proj(1, hf) for hf in range(nhalf)])


def s5_mixer(u3, h0re, h0im, w, ts):
    nbg, rows, _ = u3.shape
    rs = ts * SUBLANES
    nsteps = rows // rs
    st_spec = pl.BlockSpec((None, SUBLANES, W_S), lambda g, s: (g, 0, 0))
    nhalf = W_A // MXU_DIM
    hw = W_S // nhalf
    return pl.pallas_call(
        functools.partial(_s5_kernel, ts=ts),
        grid=(nbg, nsteps),
        in_specs=[pl.BlockSpec((None, rs, W_A), lambda g, s: (g, s, 0)),
                  pl.BlockSpec((None, rs, W_A), lambda g, s: (g, s, 1)),
                  st_spec, st_spec,
                  _const_spec((1, W_S)), _const_spec((1, W_S)),
                  _const_spec((nhalf, MXU_DIM, hw)), _const_spec((nhalf, MXU_DIM, hw)),
                  _const_spec((nhalf, hw, MXU_DIM)), _const_spec((nhalf, hw, MXU_DIM)),
                  _const_spec((1, W_A)), _const_spec((W_A, W_A)), _const_spec((1, W_A))],
        out_specs=[pl.BlockSpec((None, rs, W_A), lambda g, s: (g, s, 0)), st_spec, st_spec],
        out_shape=[jax.ShapeDtypeStruct((nbg, rows, W_A), BF16),
                   jax.ShapeDtypeStruct((nbg, SUBLANES, W_S), F32),
                   jax.ShapeDtypeStruct((nbg, SUBLANES, W_S), F32)],
        scratch_shapes=[pltpu.VMEM((rs, W_S), F32), pltpu.VMEM((rs, W_S), F32)],
        compiler_params=_cparams(("parallel", "arbitrary")),
        name="s5_mixer",
    )(u3, u3, h0re, h0im, w["s5_are"], w["s5_aim"], w["s5_wbre"], w["s5_wbim"],
      w["s5_wcre"], w["s5_wcim"], w["s5_d"], w["s5_wglu"], w["s5_bglu"])


def _conv_history(c, buf0_ref, xpad, buf1_ref, rc):
    @pl.when(c == 0)
    def _():
        xpad[0:HIST, :] = buf0_ref[...]
        buf1_ref[...] = buf0_ref[...]

    @pl.when(c > 0)
    def _():
        xpad[0:HIST, :] = xpad[rc:rc + HIST, :]


def _conv_load(live, x_ref, xpad, buf1_ref, rc):
    xpad[HIST:HIST + rc, :] = x_ref[...]
    buf1_ref[...] = jnp.where(live, xpad[rc:rc + HIST, :], buf1_ref[...])


def _conv_cols(xpad, cw_ref, cb_ref, cols, rc):
    out = cb_ref[:, cols]
    for tap in range(CONV_K):
        out = out + xpad[tap * SUBLANES:tap * SUBLANES + rc, cols] * cw_ref[tap:tap + 1, cols]
    return out


def _mlstm_kernel(x_ref, ga_ref, xb_ref, zb_ref, buf0_ref, c0_hbm, n0_ref, m0_ref, cw_ref, cb_ref,
                  wq_ref, wk_ref, wv_ref, wif_ref, bif_ref, nw_ref, skip_ref, wa_ref, wb_ref,
                  h1_ref, c1_hbm, n1_ref, m1_ref, buf1_ref,
                  xpad, xc_s, q_s, k_s, v_s, g_s, h_s, c_s, gb_s, xs, sem, *, tc, n_given):
    g = pl.program_id(0)
    c = pl.program_id(1)
    nch = pl.num_programs(1) - 1
    live = c < nch
    rc = tc * SUBLANES

    def state_copy(h, to_hbm):
        hbm = (c1_hbm if to_hbm else c0_hbm).at[pl.ds(g * SUBLANES, SUBLANES), h]
        src, dst = (c_s.at[h], hbm) if to_hbm else (hbm, c_s.at[h])
        return pltpu.make_async_copy(src, dst, sem.at[h])

    @pl.when(c == 0)
    def _():
        n1_ref[...] = n0_ref[...]
        m1_ref[...] = m0_ref[...]
        gb_s[...] = jnp.zeros(gb_s.shape, BF16)

    @pl.when(c == nch)
    def _():
        for h in range(H_B):
            state_copy(h, True).start()
        for h in range(H_B):
            state_copy(h, True).wait()

    @pl.when((c == 0) & (g < n_given))
    def _():
        for h in range(H_B):
            state_copy(h, False).start()
        for h in range(H_B):
            state_copy(h, False).wait()

    @pl.when((c == 0) & (g >= n_given))
    def _():
        c_s[...] = jnp.zeros(c_s.shape, F32)

    _conv_history(c, buf0_ref, xpad, buf1_ref, rc)

    for b in range(SUBLANES):
        _put_batch(xs, b, tc, 0, x_ref[b])

    _conv_load(live, xb_ref, xpad, buf1_ref, rc)
    gates = bif_ref[...]
    for h in range(H_B):
        cols = slice(h * DH_B, (h + 1) * DH_B)
        xc = _silu(_conv_cols(xpad, cw_ref, cb_ref, cols, rc))
        xc_s[:, cols] = xc
        xc_bf = xc.astype(BF16)
        xb_bf = xb_ref[:, cols].astype(BF16)
        for i, (src, w_ref, dst) in enumerate(((xc_bf, wq_ref, q_s), (xc_bf, wk_ref, k_s), (xb_bf, wv_ref, v_s))):
            r = _dot(src, w_ref[h])
            _put_cols(dst, h * DH_B, r)
            gates = gates + _dot(r.astype(BF16), wif_ref[i * W_B + h * DH_B:i * W_B + (h + 1) * DH_B, :])
    lane = lax.broadcasted_iota(jnp.int32, (rc, LANES), 1)
    gates = jnp.where(lane < H_B, gates, -_softplus(-gates))
    is_f = lax.broadcasted_iota(jnp.int32, (SUBLANES, LANES), 1) >= H_B
    run = jnp.zeros((SUBLANES, LANES), F32)
    for t in range(tc):
        cur = gates[t * SUBLANES:(t + 1) * SUBLANES, :]
        run = jnp.where(is_f, run + cur, cur)
        g_s[0, t * SUBLANES:(t + 1) * SUBLANES, :] = run

    ri = lax.broadcasted_iota(jnp.int32, (tc, tc), 0)
    ci = lax.broadcasted_iota(jnp.int32, (tc, tc), 1)
    causal = ri >= ci
    eye = ri == ci
    m_lane = lax.broadcasted_iota(jnp.int32, (SUBLANES, 1, LANES), 2)
    kscale = DH_B ** -0.5

    gt = _get_batches(g_s, tc, 0, LANES)
    m_old = m1_ref[...]
    m_new = m_old
    for h in range(H_B):
        cols = slice(h * DH_B, (h + 1) * DH_B)
        y_a = _dot(ga_ref[...], wa_ref[:, cols])

        ig = gt[:, :, h:h + 1]
        bc = gt[:, :, H_B + h:H_B + h + 1]
        a_row = jnp.sum(jnp.where(eye, ig - bc, 0.0), axis=1, keepdims=True)
        dlog = jnp.where(causal, bc + a_row, -jnp.inf)
        inter = bc + m_old[:, :, h:h + 1]
        m = jnp.maximum(inter, jnp.max(dlog, axis=2, keepdims=True))
        p = jnp.exp(dlog - m)
        w_inter = jnp.exp(inter - m)
        floor = jnp.exp(-m)
        m_end = m[:, tc - 1:tc, :]
        w_end = jnp.exp(bc[:, tc - 1:tc, :] - bc + ig - m_end)
        decay = jnp.exp(inter[:, tc - 1:tc, :] - m_end)
        n0 = n1_ref[:, :, cols]
        q = _get_batches(q_s, tc, h * DH_B, DH_B)
        k = _get_batches(k_s, tc, h * DH_B, DH_B) * kscale
        v_bf = _get_batches(v_s, tc, h * DH_B, DH_B).astype(BF16)
        q_bf = q.astype(BF16)
        sc = _bdot_nt(q_bf, k.astype(BF16)) * p
        c0 = c_s[h]
        num = _bdot(sc.astype(BF16), v_bf) + w_inter * _bdot(q_bf, c0.astype(BF16))
        den = jnp.sum(sc, axis=2, keepdims=True) + w_inter * jnp.sum(q * n0, axis=2, keepdims=True)
        hv = num * (1.0 / jnp.maximum(jnp.abs(den), floor))
        y_b = _dot(gb_s[...], wb_ref[:, cols])
        _put_batches(h_s, tc, h * DH_B, _head_norm(hv))
        kw = k * w_end
        c_s[h] = decay * c0 + _bdot_tn(kw.astype(BF16), v_bf)
        n1_ref[:, :, cols] = jnp.where(live, decay * n0 + jnp.sum(kw, axis=1, keepdims=True), n0)
        m_new = jnp.where(m_lane == h, m_end, m_new)
        res = jnp.concatenate([xs[j] for j in range(h * DH_B // LANES, (h + 1) * DH_B // LANES)], axis=-1)
        h1_ref[:, cols] = res + y_a + y_b
    m1_ref[...] = jnp.where(live, m_new, m_old)

    y = _get_all(h_s) * nw_ref[...] + skip_ref[...] * xc_s[...]
    gb_s[...] = (y * _silu(zb_ref[...])).astype(BF16)


def mlstm_mixer(x, ga, u3, buf0, c0, n0, m0, w, tc):
    nbg, rows, _ = u3.shape
    rc = tc * SUBLANES
    nch = rows // rc
    grp = lambda shape: pl.BlockSpec((None,) + shape, lambda g, c: (g, 0, 0))
    per_b = lambda width: pl.BlockSpec((SUBLANES, 1, width), lambda g, c: (g, 0, 0))
    cur = lambda blk: (lambda g, c: (g, jnp.minimum(c, nch - 1), blk))
    prev = lambda g, c: (g, jnp.maximum(c - 1, 0), 0)
    blocked = pltpu.VMEM((W_B // LANES, rc, LANES), F32)
    n_given = c0.shape[0] // SUBLANES
    return pl.pallas_call(
        functools.partial(_mlstm_kernel, tc=tc, n_given=n_given),
        grid=(nbg, nch + 1),
        in_specs=[pl.BlockSpec((SUBLANES, tc, D_MODEL), prev),
                  pl.BlockSpec((None, rc, W_A), prev),
                  pl.BlockSpec((None, rc, W_B), cur(1)),
                  pl.BlockSpec((None, rc, W_B), cur(2)),
                  grp((HIST, W_B)),
                  pl.BlockSpec(memory_space=pl.ANY),
                  per_b(W_B), per_b(LANES),
                  _const_spec((CONV_K, W_B)), _const_spec((1, W_B)),
                  _const_spec((H_B, DH_B, DH_B)), _const_spec((H_B, DH_B, DH_B)), _const_spec((H_B, DH_B, DH_B)),
                  _const_spec((3 * W_B, LANES)), _const_spec((1, LANES)),
                  _const_spec((1, W_B)), _const_spec((1, W_B)),
                  _const_spec((W_A, D_MODEL)), _const_spec((W_B, D_MODEL))],
        out_specs=[pl.BlockSpec((None, rc, D_MODEL), prev),
                   pl.BlockSpec(memory_space=pl.ANY),
                   per_b(W_B), per_b(LANES), grp((HIST, W_B))],
        out_shape=[jax.ShapeDtypeStruct((nbg, rows, D_MODEL), F32),
                   jax.ShapeDtypeStruct((nbg * SUBLANES,) + c0.shape[1:], F32),
                   jax.ShapeDtypeStruct((nbg * SUBLANES, 1, W_B), F32),
                   jax.ShapeDtypeStruct((nbg * SUBLANES, 1, LANES), F32),
                   jax.ShapeDtypeStruct((nbg, HIST, W_B), F32)],
        scratch_shapes=[pltpu.VMEM((rc + HIST, W_B), F32),
                        pltpu.VMEM((rc, W_B), F32), blocked, blocked, blocked,
                        pltpu.VMEM((1, rc, LANES), F32), blocked,
                        pltpu.VMEM((H_B, SUBLANES, DH_B, DH_B), F32),
                        pltpu.VMEM((rc, W_B), BF16), blocked,
                        pltpu.SemaphoreType.DMA((H_B,))],
        compiler_params=_cparams(("parallel", "arbitrary")),
        name="mlstm_mixer",
    )(x, ga, u3, u3, buf0, c0, n0, m0, w["ml_cw"], w["ml_cb"], w["ml_wq"], w["ml_wk"], w["ml_wv"],
      w["ml_wif"], w["ml_bif"], w["ml_nw"], w["ml_skip"], w["ev_w_out_a"], w["ev_w_out_b"])


def _ret_log_decay(h):
    return float(np.log1p(-np.exp2(-5.0 - h)))


OD_MIX_IN = 2 * QK_C + 2 * W_C


def _retention_kernel(h_ref, g1_ref, w_ref, cos_ref, sin_ref, s0_hbm, nw_ref,
                      gc_ref, s1_hbm, ud_ref, hn_s, slot, zc_s, q_s, k_s, v_s, o_s, st_s, sem, *, tc, n_given):
    g = pl.program_id(0)
    c = pl.program_id(1)
    started = c > 0

    @pl.when(c == 0)
    def _():
        slot[...] = jnp.zeros(slot.shape, F32)

    def state_copy(h, to_hbm):
        hbm = (s1_hbm if to_hbm else s0_hbm).at[pl.ds(g * SUBLANES, SUBLANES), h]
        src, dst = (st_s.at[h], hbm) if to_hbm else (hbm, st_s.at[h])
        return pltpu.make_async_copy(src, dst, sem.at[h])

    @pl.when((c == 0) & (g < n_given))
    def _():
        for h in range(H_C):
            state_copy(h, False).start()
        for h in range(H_C):
            state_copy(h, False).wait()

    @pl.when((c == 0) & (g >= n_given))
    def _():
        st_s[...] = jnp.zeros(st_s.shape, F32)

    hn_s[...] = _rms(h_ref[...], g1_ref[...]).astype(BF16)

    def project(j):
        cols = slice(j * MXU_DIM, (j + 1) * MXU_DIM)
        r = _dot(hn_s[...], w_ref[:, cols])
        if (j + 1) * MXU_DIM <= OD_MIX_IN:
            slot[:, cols] = r
        else:
            ud_ref[:, j * MXU_DIM - OD_MIX_IN:(j + 1) * MXU_DIM - OD_MIX_IN] = r

    lru_blocks = iter(range(OD_MIX_IN // MXU_DIM, OD_IN // MXU_DIM))
    per_copy = (OD_IN - OD_MIX_IN) // MXU_DIM // H_C
    cos, sin = cos_ref[...], sin_ref[...]
    _put_cols(v_s, 0, slot[:, 2 * QK_C:2 * QK_C + W_C])
    zc_s[...] = slot[:, 2 * QK_C + W_C:OD_MIX_IN]
    for h in range(H_C):
        for _ in range(per_copy):
            project(next(lru_blocks))
        for off, dst, scale in ((0, q_s, None), (QK_C, k_s, DK_C ** -0.5)):
            x = slot[:, off + h * DK_C:off + (h + 1) * DK_C]
            r = x * cos + pltpu.roll(x, DK_C // 2, 1) * sin
            dst[h] = r if scale is None else r * scale

    per_head = OD_MIX_IN // MXU_DIM // H_C

    ri = lax.broadcasted_iota(jnp.int32, (tc, tc), 0)
    ci = lax.broadcasted_iota(jnp.int32, (tc, tc), 1)
    diff = (ri - ci).astype(F32)
    tcol = lax.broadcasted_iota(jnp.int32, (tc, 1), 0).astype(F32)
    dmask, xi, zeta, gfull = [], [], [], []
    for h in range(H_C):
        lg = _ret_log_decay(h)
        dmask.append(jnp.where(diff >= 0, jnp.exp(lg * jnp.maximum(diff, 0.0)), 0.0))
        xi.append(jnp.exp(lg * (tcol + 1.0)))
        zeta.append(jnp.exp(lg * (tc - 1.0 - tcol)))
        gfull.append(float(np.exp(np.float32(lg) * np.float32(tc))))

    for h in range(H_C):
        nxt = iter(range(h * per_head, (h + 1) * per_head))
        q_bf = _get_batches(q_s, tc, h * DK_C, DK_C).astype(BF16)
        k = _get_batches(k_s, tc, h * DK_C, DK_C)
        v_bf = _get_batches(v_s, tc, h * DV_C, DV_C).astype(BF16)
        s0 = st_s[h]
        inner = _bdot_nt(q_bf, k.astype(BF16)) * dmask[h]
        project(next(nxt))
        o = _bdot(inner.astype(BF16), v_bf) + xi[h] * _bdot(q_bf, s0.astype(BF16))
        project(next(nxt))
        _put_batches(o_s, tc, h * DV_C, _head_norm(o))
        decay = jnp.where(started, gfull[h], 1.0)
        st_s[h] = decay * s0 + _bdot_tn((k * zeta[h]).astype(BF16), v_bf)
        for j in nxt:
            project(j)

    gc_ref[...] = (_get_all(o_s) * nw_ref[...] * _silu(zc_s[...])).astype(BF16)

    @pl.when(c == pl.num_programs(1) - 1)
    def _():
        for h in range(H_C):
            state_copy(h, True).start()
        for h in range(H_C):
            state_copy(h, True).wait()


def retention_mixer(h1, cos, sin, s0, w, tc):
    nbg, rows, d = h1.shape
    rc = tc * SUBLANES
    nch = rows // rc
    assert OD_MIX_IN % (MXU_DIM * H_C) == 0 and (OD_IN - OD_MIX_IN) % (MXU_DIM * H_C) == 0
    n_given = s0.shape[0] // SUBLANES
    cur = lambda g, c: (g, jnp.minimum(c, nch - 1), 0)
    prev = lambda g, c: (g, jnp.maximum(c - 1, 0), 0)
    return pl.pallas_call(
        functools.partial(_retention_kernel, tc=tc, n_given=n_given),
        grid=(nbg, nch + 1),
        in_specs=[pl.BlockSpec((None, rc, d), cur),
                  _const_spec((1, d)), _const_spec((d, OD_IN)),
                  pl.BlockSpec((None, rc, DK_C), prev),
                  pl.BlockSpec((None, rc, DK_C), prev),
                  pl.BlockSpec(memory_space=pl.ANY),
                  _const_spec((1, W_C))],
        out_specs=[pl.BlockSpec((None, rc, W_C), prev),
                   pl.BlockSpec(memory_space=pl.ANY),
                   pl.BlockSpec((None, rc, OD_IN - OD_MIX_IN), cur)],
        out_shape=[jax.ShapeDtypeStruct((nbg, rows, W_C), BF16),
                   jax.ShapeDtypeStruct((nbg * SUBLANES,) + s0.shape[1:], F32),
                   jax.ShapeDtypeStruct((nbg, rows, OD_IN - OD_MIX_IN), F32)],
        scratch_shapes=[pltpu.VMEM((rc, d), BF16), pltpu.VMEM((rc, OD_MIX_IN), F32), pltpu.VMEM((rc, W_C), F32),
                        pltpu.VMEM((QK_C // LANES, rc, LANES), F32), pltpu.VMEM((QK_C // LANES, rc, LANES), F32),
                        pltpu.VMEM((W_C // LANES, rc, LANES), F32), pltpu.VMEM((W_C // LANES, rc, LANES), F32),
                        pltpu.VMEM((H_C, SUBLANES, DK_C, DV_C), F32),
                        pltpu.SemaphoreType.DMA((H_C,))],
        compiler_params=_cparams(("parallel", "arbitrary")),
        name="retention_mixer",
    )(h1, w["norm1"], w["od_w_in"], cos, sin, s0, w["ret_nw"])


def _rglru_kernel(res_ref, gc_ref, xd_ref, zd_ref, buf0_ref, h0_ref, cw_ref, cb_ref, wa_ref, ba_ref, wx_ref,
                  bx_ref, lam_ref, woc_ref, wod_ref, fn_ref,
                  y_ref, h1_ref, buf1_ref, xpad, a_s, b_s, gd_s, ys, inv_s, *, tc):
    c = pl.program_id(1)
    nch = pl.num_programs(1) - 1
    live = c < nch
    rc = tc * SUBLANES

    @pl.when(c == 0)
    def _():
        h1_ref[...] = h0_ref[...]
        gd_s[...] = jnp.zeros(gd_s.shape, BF16)

    _conv_history(c, buf0_ref, xpad, buf1_ref, rc)

    xpad[HIST:HIST + rc, :] = xd_ref[...]
    buf1_ref[...] = jnp.where(live, xpad[rc:rc + HIST, :], buf1_ref[...])
    nsp = (-0.5 * LRU_C) * _softplus(-lam_ref[...])
    ssq = jnp.zeros((rc, 1), F32)
    for j in range(W_D // MXU_DIM):
        cols = slice(j * MXU_DIM, (j + 1) * MXU_DIM)
        y_c = _dot(gc_ref[...], woc_ref[:, cols])
        xc = cb_ref[:, cols]
        for tap in range(CONV_K):
            xc = xc + xpad[tap * SUBLANES:tap * SUBLANES + rc, cols] * cw_ref[tap:tap + 1, cols]
        lhs = xc.astype(BF16)
        t_r = jnp.tanh(0.5 * (_dot(lhs, wa_ref[j]) + ba_ref[:, cols]))
        y_d = _dot(gd_s[...], wod_ref[:, cols])
        i = _sigmoid(_dot(lhs, wx_ref[j]) + bx_ref[:, cols])
        log_a = nsp[:, cols] * t_r + nsp[:, cols]
        a = jnp.exp(log_a)
        a_s[:, cols] = a
        b_s[:, cols] = jnp.sqrt(-jnp.tanh(log_a) * (a * a + 1.0)) * (i * xc)
        yj = res_ref[:, cols] + y_c + y_d
        _put_cols(ys, j * MXU_DIM, yj)
        ssq = ssq + jnp.sum(yj * yj, axis=-1, keepdims=True)

    inv_s[0] = jnp.broadcast_to(lax.rsqrt(ssq * (1.0 / D_MODEL) + EPS), (rc, LANES))
    for b in range(SUBLANES):
        inv_b = _get_batch(inv_s, b, tc, 0, LANES)[:, 0:1]
        y_ref[b] = _get_batch(ys, b, tc, 0, D_MODEL) * inv_b * fn_ref[...]

    def step(t, h):
        r0 = pl.multiple_of(t * SUBLANES, SUBLANES)
        h = a_s[pl.ds(r0, SUBLANES), :] * h + b_s[pl.ds(r0, SUBLANES), :]
        b_s[pl.ds(r0, SUBLANES), :] = h
        return h

    h_old = h1_ref[...]
    h1_ref[...] = jnp.where(live, lax.fori_loop(0, tc, step, h_old, unroll=4), h_old)
    gd_s[...] = (b_s[...] * _silu(zd_ref[...])).astype(BF16)


def rglru_mixer(res, gc, u3, buf0, h0, w, tc):
    nbg, rows, _ = u3.shape
    rc = tc * SUBLANES
    nch = rows // rc
    nblk = W_D // MXU_DIM
    grp = lambda shape: pl.BlockSpec((None,) + shape, lambda g, c: (g, 0, 0))
    cur = lambda blk: (lambda g, c: (g, jnp.minimum(c, nch - 1), blk))
    prev = lambda g, c: (g, jnp.maximum(c - 1, 0), 0)
    return pl.pallas_call(
        functools.partial(_rglru_kernel, tc=tc),
        grid=(nbg, nch + 1),
        in_specs=[pl.BlockSpec((None, rc, D_MODEL), prev),
                  pl.BlockSpec((None, rc, W_C), prev),
                  pl.BlockSpec((None, rc, W_D), cur(0)),
                  pl.BlockSpec((None, rc, W_D), cur(1)),
                  grp((HIST, W_D)), grp((SUBLANES, W_D)),
                  _const_spec((CONV_K, W_D)), _const_spec((1, W_D)),
                  _const_spec((nblk, MXU_DIM, MXU_DIM)), _const_spec((1, W_D)),
                  _const_spec((nblk, MXU_DIM, MXU_DIM)), _const_spec((1, W_D)),
                  _const_spec((1, W_D)),
                  _const_spec((W_C, D_MODEL)), _const_spec((W_D, D_MODEL)), _const_spec((1, D_MODEL))],
        out_specs=[pl.BlockSpec((SUBLANES, tc, D_MODEL), prev),
                   grp((SUBLANES, W_D)), grp((HIST, W_D))],
        out_shape=[jax.ShapeDtypeStruct((nbg * SUBLANES, rows // SUBLANES, D_MODEL), F32),
                   jax.ShapeDtypeStruct((nbg, SUBLANES, W_D), F32),
                   jax.ShapeDtypeStruct((nbg, HIST, W_D), F32)],
        scratch_shapes=[pltpu.VMEM((rc + HIST, W_D), F32), pltpu.VMEM((rc, W_D), F32), pltpu.VMEM((rc, W_D), F32),
                        pltpu.VMEM((rc, W_D), BF16), pltpu.VMEM((D_MODEL // LANES, rc, LANES), F32),
                        pltpu.VMEM((1, rc, LANES), F32)],
        compiler_params=_cparams(("parallel", "arbitrary")),
        name="rglru_mixer",
    )(res, gc, u3, u3, buf0, h0, w["lru_cw"], w["lru_cb"], w["lru_wa"], w["lru_ba"], w["lru_wx"], w["lru_bx"],
      w["lru_lam"], w["od_w_out_c"], w["od_w_out_d"], w["final_norm"])


def _to_tm(x):
    b, t, c = x.shape
    return x.reshape(b // SUBLANES, SUBLANES, t, c).swapaxes(1, 2).reshape(b // SUBLANES, t * SUBLANES, c)


def _from_tm(x, t):
    nbg, _, c = x.shape
    return x.reshape(nbg, t, SUBLANES, c).swapaxes(1, 2).reshape(nbg * SUBLANES, t, c)


def _rope_tables(pos):
    half = DK_C // 2
    inv = ROPE_BASE ** (-jnp.arange(half, dtype=F32) / half)
    ang = pos[..., None] * inv
    cos = jnp.cos(ang)
    sin = jnp.sin(ang)
    cos = jnp.concatenate([cos, cos], axis=-1)
    sin = jnp.concatenate([-sin, sin], axis=-1)
    rep = lambda a: jnp.repeat(a, SUBLANES, axis=1)
    return rep(cos), rep(sin)


def _prep_weights(p):
    w = {}
    w["norm0"] = p["norm_w"][0][None]
    w["norm1"] = p["norm_w"][1][None]
    w["final_norm"] = p["final_norm_w"][None]
    w["ev_w_in"] = p["ev_w_in"][0].astype(BF16)
    w["ev_w_out_a"] = p["ev_w_out"][0][:W_A].astype(BF16)
    w["ev_w_out_b"] = p["ev_w_out"][0][W_A:].astype(BF16)
    are, aim, bbre, bbim = s5_prep(p["s5_lambda_re"][0], p["s5_lambda_im"][0], p["s5_log_dt"][0],
                                   p["s5_b_re"][0], p["s5_b_im"][0])
    per = MXU_DIM // S5_GROUP
    w["s5_are"] = are.reshape(1, W_S)
    w["s5_aim"] = aim.reshape(1, W_S)
    w["s5_wbre"] = _block_diag(bbre, per).astype(BF16)
    w["s5_wbim"] = _block_diag(bbim, per).astype(BF16)
    w["s5_wcre"] = _block_diag(jnp.swapaxes(p["s5_c_re"][0], 1, 2), per).astype(BF16)
    w["s5_wcim"] = _block_diag(-jnp.swapaxes(p["s5_c_im"][0], 1, 2), per).astype(BF16)
    w["s5_d"] = p["s5_d"][0][None]
    w["s5_wglu"] = p["s5_w_glu"][0].astype(BF16)
    w["s5_bglu"] = p["s5_b_glu"][0][None]
    w["ml_cw"] = p["ml_conv_w"][0]
    w["ml_cb"] = p["ml_conv_b"][0][None]
    w["ml_wq"] = p["ml_wq"][0].astype(BF16)
    w["ml_wk"] = p["ml_wk"][0].astype(BF16)
    w["ml_wv"] = p["ml_wv"][0].astype(BF16)
    w["ml_wif"] = jnp.pad(p["ml_w_if"][0], ((0, 0), (0, LANES - 2 * H_B))).astype(BF16)
    w["ml_bif"] = jnp.pad(p["ml_b_if"][0], (0, LANES - 2 * H_B))[None]
    w["ml_nw"] = p["ml_norm_w"][0][None]
    w["ml_skip"] = p["ml_skip"][0][None]
    w["od_w_in"] = p["od_w_in"][0].astype(BF16)
    w["od_w_out_c"] = p["od_w_out"][0][:W_C].astype(BF16)
    w["od_w_out_d"] = p["od_w_out"][0][W_C:].astype(BF16)
    w["ret_nw"] = p["ret_norm_w"][0][None]
    w["lru_cw"] = p["lru_conv_w"][0]
    w["lru_cb"] = p["lru_conv_b"][0][None]
    w["lru_wa"] = _block_diag(p["lru_w_a"][0], MXU_DIM // BD_D).astype(BF16)
    w["lru_ba"] = p["lru_b_a"][0][None]
    w["lru_wx"] = _block_diag(p["lru_w_x"][0], MXU_DIM // BD_D).astype(BF16)
    w["lru_bx"] = p["lru_b_x"][0][None]
    w["lru_lam"] = p["lru_lambda"][0][None]
    return w


def _conv_to_tm(buf):
    return _to_tm(buf)


def _trunk(x, pos, st, w, tc):
    new = {}
    u = proj_in(x, w["norm0"], w["ev_w_in"], True)
    ga, new["s5_re"], new["s5_im"] = s5_mixer(u, st["s5_re"], st["s5_im"], w, tc)
    h1, new["ml_c"], new["ml_n"], new["ml_m"], new["ml_conv"] = mlstm_mixer(
        x, ga, u, st["ml_conv"], st["ml_c"], st["ml_n"], st["ml_m"], w, tc)
    cos, sin = _rope_tables(pos)
    gc, new["ret"], ud = retention_mixer(h1, cos, sin, st["ret"], w, tc)
    y, new["lru_h"], new["lru_conv"] = rglru_mixer(h1, gc, ud, st["lru_conv"], st["lru_h"], w, tc)
    return y, new


def _states_in(s5_re, s5_im, ml_c, ml_n, ml_m, ml_conv, ret, lru_h, lru_conv):
    b = s5_re.shape[0]
    nbg = b // SUBLANES
    return dict(
        s5_re=s5_re.reshape(nbg, SUBLANES, W_S),
        s5_im=s5_im.reshape(nbg, SUBLANES, W_S),
        ml_c=ml_c,
        ml_n=ml_n.reshape(b, 1, W_B),
        ml_m=jnp.pad(ml_m, ((0, 0), (0, LANES - H_B))).reshape(b, 1, LANES),
        ml_conv=_conv_to_tm(ml_conv),
        ret=ret,
        lru_h=lru_h.reshape(nbg, SUBLANES, W_D),
        lru_conv=_conv_to_tm(lru_conv),
    )


def _states_out(st, sel=None):
    nbg = st["s5_re"].shape[0]
    b = nbg * SUBLANES
    out = dict(
        s5_re=st["s5_re"].reshape(b, G_A, S5_STATE),
        s5_im=st["s5_im"].reshape(b, G_A, S5_STATE),
        ml_c=st["ml_c"].reshape(b, H_B, DH_B, DH_B),
        ml_n=st["ml_n"].reshape(b, H_B, DH_B),
        ml_m=st["ml_m"].reshape(b, LANES)[:, :H_B],
        ml_conv=_from_tm(st["ml_conv"], CONV_K - 1),
        ret=st["ret"].reshape(b, H_C, DK_C, DV_C),
        lru_h=st["lru_h"].reshape(b, W_D),
        lru_conv=_from_tm(st["lru_conv"], CONV_K - 1),
    )
    if sel is not None:
        out = {k: v[sel] for k, v in out.items()}
    return out


_STATE_ORDER = ("s5_re", "s5_im", "ml_c", "ml_n", "ml_m", "ml_conv", "ret", "lru_h", "lru_conv")


def kernel(x_prompt, x_sample, state_s5_re, state_s5_im, state_ml_c, state_ml_n, state_ml_m, state_ml_conv, state_ret, state_lru_h, state_lru_conv, meta, norm_w, final_norm_w, ev_w_in, ev_w_out, s5_lambda_re, s5_lambda_im, s5_log_dt, s5_b_re, s5_b_im, s5_c_re, s5_c_im, s5_d, s5_w_glu, s5_b_glu, ml_conv_w, ml_conv_b, ml_wq, ml_wk, ml_wv, ml_w_if, ml_b_if, ml_norm_w, ml_skip, od_w_in, od_w_out, ret_norm_w, lru_conv_w, lru_conv_b, lru_w_a, lru_b_a, lru_w_x, lru_b_x, lru_lambda):
    p = dict(norm_w=norm_w, final_norm_w=final_norm_w, ev_w_in=ev_w_in, ev_w_out=ev_w_out,
             s5_lambda_re=s5_lambda_re, s5_lambda_im=s5_lambda_im, s5_log_dt=s5_log_dt,
             s5_b_re=s5_b_re, s5_b_im=s5_b_im, s5_c_re=s5_c_re, s5_c_im=s5_c_im, s5_d=s5_d,
             s5_w_glu=s5_w_glu, s5_b_glu=s5_b_glu, ml_conv_w=ml_conv_w, ml_conv_b=ml_conv_b,
             ml_wq=ml_wq, ml_wk=ml_wk, ml_wv=ml_wv, ml_w_if=ml_w_if, ml_b_if=ml_b_if,
             ml_norm_w=ml_norm_w, ml_skip=ml_skip, od_w_in=od_w_in, od_w_out=od_w_out,
             ret_norm_w=ret_norm_w, lru_conv_w=lru_conv_w, lru_conv_b=lru_conv_b,
             lru_w_a=lru_w_a, lru_b_a=lru_b_a, lru_w_x=lru_w_x, lru_b_x=lru_b_x, lru_lambda=lru_lambda)
    assert ev_w_in.shape[0] == 1 and od_w_in.shape[0] == 1, "two-layer trunk"
    w = _prep_weights(p)
    bp, tp, _ = x_prompt.shape
    bs, ts, _ = x_sample.shape
    assert ts == N_META and bp % SUBLANES == 0 and bs % SUBLANES == 0 and tp % CHUNK == 0

    meta_b = jnp.broadcast_to(meta[None], (bp, N_META, D_MODEL))
    x_short = jnp.concatenate([x_sample, meta_b], axis=0)
    given = dict(s5_re=state_s5_re[0], s5_im=state_s5_im[0], ml_c=state_ml_c[0], ml_n=state_ml_n[0],
                 ml_m=state_ml_m[0], ml_conv=state_ml_conv[0], ret=state_ret[0], lru_h=state_lru_h[0],
                 lru_conv=state_lru_conv[0])
    both = {k: s if k in ("ml_c", "ret") else jnp.concatenate([s, jnp.zeros((bp,) + s.shape[1:], s.dtype)], axis=0)
            for k, s in given.items()}
    tpos = jnp.arange(N_META, dtype=F32)
    pos_short = jnp.concatenate([jnp.broadcast_to((N_META + PAST_LEN) + tpos, (bs // SUBLANES, N_META)),
                                 jnp.broadcast_to(tpos, (bp // SUBLANES, N_META))], axis=0)
    y_short, st_short = _trunk(x_short, pos_short, _states_in(**both), w, N_META)
    y_sample = y_short[:bs]
    new_s = _states_out(st_short, slice(0, bs))

    nbs = bs // SUBLANES
    nbg_short = (bs + bp) // SUBLANES
    st_meta = {}
    for name, val in st_short.items():
        per_group = val.shape[0] // nbg_short
        st_meta[name] = val[nbs * per_group:]
    pos_long = jnp.broadcast_to(N_META + jnp.arange(tp, dtype=F32), (bp // SUBLANES, tp))
    y_prompt, st_long = _trunk(x_prompt, pos_long, st_meta, w, CHUNK)
    new_p = _states_out(st_long)

    return ((y_prompt, y_sample)
            + tuple(new_p[k][None] for k in _STATE_ORDER)
            + tuple(new_s[k][None] for k in _STATE_ORDER))
```

```python
import functools
import math

import numpy as np
import jax
import jax.numpy as jnp
from jax import lax
from jax.experimental import pallas as pl
from jax.experimental.pallas import tpu as pltpu

F32 = jnp.float32
BF16 = jnp.bfloat16

SUBLANES = 8
LANES = 128
MXU_DIM = 256
VMEM_LIMIT = 60 * 1024 * 1024

D_MODEL = 1024
CHUNK = 64
N_META = 16
PAST_LEN = 2048
EPS = 1e-6
CONV_K = 4
S5_GROUP = 16
S5_STATE = 64
W_A = D_MODEL // 2
G_A = W_A // S5_GROUP
W_S = G_A * S5_STATE
W_B = D_MODEL
H_B = 4
DH_B = W_B // H_B
W_C = D_MODEL
H_C = 4
DV_C = W_C // H_C
DK_C = DV_C // 2
QK_C = H_C * DK_C
ROPE_BASE = 10000.0
W_D = D_MODEL
H_D = 8
BD_D = W_D // H_D
LRU_C = 8.0
EV_IN = 2 * W_A + 2 * W_B
OD_IN = 2 * QK_C + 2 * W_C + 2 * W_D
HIST = (CONV_K - 1) * SUBLANES

ROW_TILE = 1024
S5_COLS = 512


def _cparams(sem):
    return pltpu.CompilerParams(dimension_semantics=sem, vmem_limit_bytes=VMEM_LIMIT)


def _const_spec(shape):
    nd = len(shape)
    return pl.BlockSpec(shape, lambda *_: (0,) * nd, pipeline_mode=pl.Buffered(1))


def _dot(a, b):
    return jnp.dot(a, b, preferred_element_type=F32)


def _dot_nt(a, b):
    return lax.dot_general(a, b, (((1,), (1,)), ((), ())), preferred_element_type=F32)


def _dot_tn(a, b):
    return lax.dot_general(a, b, (((0,), (0,)), ((), ())), preferred_element_type=F32)


def _rms(x, g):
    return x * lax.rsqrt(jnp.mean(x * x, axis=-1, keepdims=True) + EPS) * g


def _softplus(x):
    return jnp.maximum(x, 0.0) + jnp.log1p(jnp.exp(-jnp.abs(x)))


def _sigmoid(x):
    return 0.5 * jnp.tanh(0.5 * x) + 0.5


def _silu(x):
    h = 0.5 * x
    return h * jnp.tanh(h) + h


def _put_cols(dst3, col0, val):
    for j in range(val.shape[1] // LANES):
        dst3[col0 // LANES + j] = val[:, j * LANES:(j + 1) * LANES]


def _get_all(src3):
    return jnp.concatenate([src3[j] for j in range(src3.shape[0])], axis=-1)


def _get_batch(src3, b, tc, col0, width):
    parts = [src3[col0 // LANES + j, pl.ds(b, tc, stride=SUBLANES), :] for j in range(width // LANES)]
    return parts[0] if len(parts) == 1 else jnp.concatenate(parts, axis=-1)


def _put_batch(dst3, b, tc, col0, val):
    for j in range(val.shape[1] // LANES):
        dst3[col0 // LANES + j, pl.ds(b, tc, stride=SUBLANES), :] = val[:, j * LANES:(j + 1) * LANES]


def _get_batches(src3, tc, col0, width):
    return jnp.stack([_get_batch(src3, b, tc, col0, width) for b in range(SUBLANES)], axis=0)


def _put_batches(dst3, tc, col0, val):
    for b in range(SUBLANES):
        _put_batch(dst3, b, tc, col0, val[b])


def _bdot(a, b):
    return lax.dot_general(a, b, (((2,), (1,)), ((0,), (0,))), preferred_element_type=F32)


def _bdot_nt(a, b):
    return lax.dot_general(a, b, (((2,), (2,)), ((0,), (0,))), preferred_element_type=F32)


def _bdot_tn(a, b):
    return lax.dot_general(a, b, (((1,), (1,)), ((0,), (0,))), preferred_element_type=F32)


def _head_norm(h):
    mu = jnp.mean(h, axis=-1, keepdims=True)
    d = h - mu
    var = jnp.mean(d * d, axis=-1, keepdims=True)
    return d * lax.rsqrt(var + EPS)


def _load_bm_as_tm(x_ref, xs):
    tt = x_ref.shape[1]
    for b in range(SUBLANES):
        _put_batch(xs, b, tt, 0, x_ref[b])
    return _get_all(xs)


def _act_spec(bm, tt, width):
    if bm:
        return pl.BlockSpec((SUBLANES, tt, width), lambda g, i: (g, i, 0))
    return pl.BlockSpec((None, tt * SUBLANES, width), lambda g, i: (g, i, 0))


def _act_dims(x, bm):
    return (x.shape[0] // SUBLANES, x.shape[1]) if bm else (x.shape[0], x.shape[1] // SUBLANES)


def _proj_in_kernel(x_ref, g_ref, w_ref, o_ref, *scratch, x_bm):
    x = _load_bm_as_tm(x_ref, scratch[0]) if x_bm else x_ref[...]
    o_ref[...] = _dot(_rms(x, g_ref[...]).astype(BF16), w_ref[...])


def proj_in(x, g, w, x_bm):
    d, n = w.shape
    nbg, t = _act_dims(x, x_bm)
    tt = min(ROW_TILE // SUBLANES, t)
    rows = tt * SUBLANES
    return pl.pallas_call(
        functools.partial(_proj_in_kernel, x_bm=x_bm),
        grid=(nbg, t // tt),
        in_specs=[_act_spec(x_bm, tt, d), _const_spec((1, d)), _const_spec((d, n))],
        out_specs=_act_spec(False, tt, n),
        out_shape=jax.ShapeDtypeStruct((nbg, t * SUBLANES, n), F32),
        scratch_shapes=[pltpu.VMEM((d // LANES, rows, LANES), F32)] if x_bm else [],
        compiler_params=_cparams(("parallel", "parallel")),
        name="proj_in",
    )(x, g, w)


def _s5_prep_kernel(lre_ref, lim_ref, ldt_ref, bre_ref, bim_ref, are_ref, aim_ref, bbre_ref, bbim_ref):
    lre, lim = lre_ref[...], lim_ref[...]
    dt = jnp.exp(ldt_ref[...])
    mag = jnp.exp(lre * dt)
    ab_re = mag * jnp.cos(lim * dt)
    ab_im = mag * jnp.sin(lim * dt)
    den = lre * lre + lim * lim
    nr = ab_re - 1.0
    k_re = (nr * lre + ab_im * lim) / den
    k_im = (ab_im * lre - nr * lim) / den
    bre, bim = bre_ref[...], bim_ref[...]
    are_ref[...] = ab_re
    aim_ref[...] = ab_im
    bbre_ref[...] = k_re * bre - k_im * bim
    bbim_ref[...] = k_re * bim + k_im * bre


def s5_prep(lam_re, lam_im, log_dt, b_re, b_im):
    rep = lambda a: jnp.repeat(a, S5_GROUP, axis=0)
    rows = G_A * S5_GROUP
    ldt = jnp.broadcast_to(rep(log_dt[:, None]), (rows, S5_STATE))
    tr = lambda b: jnp.swapaxes(b, 1, 2).reshape(rows, S5_STATE)
    outs = pl.pallas_call(
        _s5_prep_kernel,
        out_shape=[jax.ShapeDtypeStruct((rows, S5_STATE), F32)] * 4,
        name="s5_prep",
    )(rep(lam_re), rep(lam_im), ldt, tr(b_re), tr(b_im))
    are, aim, bbre, bbim = (o.reshape(G_A, S5_GROUP, S5_STATE) for o in outs)
    return are[:, 0], aim[:, 0], bbre, bbim


def _block_diag(blocks, per):
    n, r, c = blocks.shape
    b = blocks.reshape(n // per, per, r, c)
    eye = jnp.eye(per, dtype=blocks.dtype)
    return jnp.einsum("hgrc,gk->hgrkc", b, eye).reshape(n // per, per * r, per * c)


def _s5_kernel(ua_ref, za_ref, h0re_ref, h0im_ref, are_ref, aim_ref, wbre_ref, wbim_ref,
               wcre_ref, wcim_ref, d_ref, wglu_ref, bglu_ref,
               ga_ref, sre_ref, sim_ref, bre_s, bim_s, *, ts):
    s = pl.program_id(1)

    @pl.when(s == 0)
    def _():
        sre_ref[...] = h0re_ref[...]
        sim_ref[...] = h0im_ref[...]

    nhalf = W_A // MXU_DIM
    hw = W_S // nhalf
    ncb = W_S // S5_COLS
    nsub = 2
    th = ts // nsub
    rsub = th * SUBLANES

    def rows(sb):
        return slice(sb * rsub, (sb + 1) * rsub)

    def b_proj(sb, i):
        hf, dst, w = i // 2, (bre_s, bim_s)[i % 2], (wbre_ref, wbim_ref)[i % 2]
        lhs = ua_ref[rows(sb), hf * MXU_DIM:(hf + 1) * MXU_DIM].astype(BF16)
        dst[rows(sb), hf * hw:(hf + 1) * hw] = _dot(lhs, w[hf])

    def scan(sb, cb):
        cols = slice(cb * S5_COLS, (cb + 1) * S5_COLS)
        ar = jnp.broadcast_to(are_ref[:, cols], (SUBLANES, S5_COLS))
        ai = jnp.broadcast_to(aim_ref[:, cols], (SUBLANES, S5_COLS))
        hr, hi = sre_ref[:, cols], sim_ref[:, cols]
        for t in range(th):
            r = slice(sb * rsub + t * SUBLANES, sb * rsub + (t + 1) * SUBLANES)
            hr, hi = ar * hr - ai * hi + bre_s[r, cols], ar * hi + ai * hr + bim_s[r, cols]
            bre_s[r, cols] = hr
            bim_s[r, cols] = hi
        sre_ref[:, cols] = hr
        sim_ref[:, cols] = hi

    def c_proj(sb, hf):
        hre = bre_s[rows(sb), hf * hw:(hf + 1) * hw].astype(BF16)
        him = bim_s[rows(sb), hf * hw:(hf + 1) * hw].astype(BF16)
        return _dot(hre, wcre_ref[hf]) + _dot(him, wcim_ref[hf])

    def finish(sb, ys):
        y = jnp.concatenate(ys, axis=-1) + d_ref[...] * ua_ref[rows(sb), :]
        y = jax.nn.gelu(y, approximate=True)
        y = y * _sigmoid(_dot(y.astype(BF16), wglu_ref[...]) + bglu_ref[...])
        ga_ref[rows(sb), :] = (y * _silu(za_ref[rows(sb), :])).astype(BF16)

    assert 2 * nhalf == ncb and nhalf == 2
    for i in range(2 * nhalf):
        b_proj(0, i)
    for i in range(ncb):
        b_proj(1, i)
        scan(0, i)
    ys = []
    for hf in range(nhalf):
        ys.append(c_proj(0, hf))
        scan(1, 2 * hf)
    finish(0, ys)
    scan(1, 1)
    scan(1, 3)
    finish(1, [c_proj(1, hf) for hf in range(nhalf)])


def s5_mixer(u3, h0re, h0im, w, ts):
    nbg, rows, _ = u3.shape
    rs = ts * SUBLANES
    nsteps = rows // rs
    st_spec = pl.BlockSpec((None, SUBLANES, W_S), lambda g, s: (g, 0, 0))
    nhalf = W_A // MXU_DIM
    hw = W_S // nhalf
    return pl.pallas_call(
        functools.partial(_s5_kernel, ts=ts),
        grid=(nbg, nsteps),
        in_specs=[pl.BlockSpec((None, rs, W_A), lambda g, s: (g, s, 0)),
                  pl.BlockSpec((None, rs, W_A), lambda g, s: (g, s, 1)),
                  st_spec, st_spec,
                  _const_spec((1, W_S)), _const_spec((1, W_S)),
                  _const_spec((nhalf, MXU_DIM, hw)), _const_spec((nhalf, MXU_DIM, hw)),
                  _const_spec((nhalf, hw, MXU_DIM)), _const_spec((nhalf, hw, MXU_DIM)),
                  _const_spec((1, W_A)), _const_spec((W_A, W_A)), _const_spec((1, W_A))],
        out_specs=[pl.BlockSpec((None, rs, W_A), lambda g, s: (g, s, 0)), st_spec, st_spec],
        out_shape=[jax.ShapeDtypeStruct((nbg, rows, W_A), BF16),
                   jax.ShapeDtypeStruct((nbg, SUBLANES, W_S), F32),
                   jax.ShapeDtypeStruct((nbg, SUBLANES, W_S), F32)],
        scratch_shapes=[pltpu.VMEM((rs, W_S), F32), pltpu.VMEM((rs, W_S), F32)],
        compiler_params=_cparams(("parallel", "arbitrary")),
        name="s5_mixer",
    )(u3, u3, h0re, h0im, w["s5_are"], w["s5_aim"], w["s5_wbre"], w["s5_wbim"],
      w["s5_wcre"], w["s5_wcim"], w["s5_d"], w["s5_wglu"], w["s5_bglu"])


def _conv_history(c, buf0_ref, xpad, buf1_ref, rc):
    @pl.when(c == 0)
    def _():
        xpad[0:HIST, :] = buf0_ref[...]
        buf1_ref[...] = buf0_ref[...]

    @pl.when(c > 0)
    def _():
        xpad[0:HIST, :] = xpad[rc:rc + HIST, :]


def _conv_load(live, x_ref, xpad, buf1_ref, rc):
    xpad[HIST:HIST + rc, :] = x_ref[...]
    buf1_ref[...] = jnp.where(live, xpad[rc:rc + HIST, :], buf1_ref[...])


def _conv_cols(xpad, cw_ref, cb_ref, cols, rc):
    out = cb_ref[:, cols]
    for tap in range(CONV_K):
        out = out + xpad[tap * SUBLANES:tap * SUBLANES + rc, cols] * cw_ref[tap:tap + 1, cols]
    return out


def _mlstm_kernel(x_ref, ga_ref, xb_ref, zb_ref, buf0_ref, c0_hbm, n0_ref, m0_ref, cw_ref, cb_ref,
                  wq_ref, wk_ref, wv_ref, wif_ref, bif_ref, nw_ref, skip_ref, wa_ref, wb_ref,
                  h1_ref, c1_hbm, n1_ref, m1_ref, buf1_ref,
                  xpad, xc_s, q_s, k_s, v_s, g_s, h_s, c_s, gb_s, xs, sem, *, tc, n_given, g_off):
    g = pl.program_id(0)
    c = pl.program_id(1)
    nch = pl.num_programs(1) - 1
    live = c < nch
    rc = tc * SUBLANES

    def state_copy(h, to_hbm):
        hbm = (c1_hbm.at[pl.ds(g * SUBLANES, SUBLANES), h] if to_hbm else
               c0_hbm.at[pl.ds((g + g_off) * SUBLANES, SUBLANES), h])
        src, dst = (c_s.at[h], hbm) if to_hbm else (hbm, c_s.at[h])
        return pltpu.make_async_copy(src, dst, sem.at[h])

    @pl.when(c == 0)
    def _():
        n1_ref[...] = n0_ref[...]
        m1_ref[...] = m0_ref[...]
        gb_s[...] = jnp.zeros(gb_s.shape, BF16)

    @pl.when(c == nch)
    def _():
        for h in range(H_B):
            state_copy(h, True).start()
        for h in range(H_B):
            state_copy(h, True).wait()

    @pl.when((c == 0) & (g < n_given))
    def _():
        for h in range(H_B):
            state_copy(h, False).start()
        for h in range(H_B):
            state_copy(h, False).wait()

    @pl.when((c == 0) & (g >= n_given))
    def _():
        c_s[...] = jnp.zeros(c_s.shape, F32)

    _conv_history(c, buf0_ref, xpad, buf1_ref, rc)

    for b in range(SUBLANES):
        _put_batch(xs, b, tc, 0, x_ref[b])

    _conv_load(live, xb_ref, xpad, buf1_ref, rc)
    gates = bif_ref[...]
    for h in range(H_B):
        cols = slice(h * DH_B, (h + 1) * DH_B)
        xc = _silu(_conv_cols(xpad, cw_ref, cb_ref, cols, rc))
        xc_s[:, cols] = xc
        xc_bf = xc.astype(BF16)
        xb_bf = xb_ref[:, cols].astype(BF16)
        for i, (src, w_ref, dst) in enumerate(((xc_bf, wq_ref, q_s), (xc_bf, wk_ref, k_s), (xb_bf, wv_ref, v_s))):
            r = _dot(src, w_ref[h])
            _put_cols(dst, h * DH_B, r)
            gates = gates + _dot(r.astype(BF16), wif_ref[i * W_B + h * DH_B:i * W_B + (h + 1) * DH_B, :])
    lane = lax.broadcasted_iota(jnp.int32, (rc, LANES), 1)
    gates = jnp.where(lane < H_B, gates, -_softplus(-gates))
    is_f = lax.broadcasted_iota(jnp.int32, (SUBLANES, LANES), 1) >= H_B
    run = jnp.zeros((SUBLANES, LANES), F32)
    for t in range(tc):
        cur = gates[t * SUBLANES:(t + 1) * SUBLANES, :]
        run = jnp.where(is_f, run + cur, cur)
        g_s[0, t * SUBLANES:(t + 1) * SUBLANES, :] = run

    ri = lax.broadcasted_iota(jnp.int32, (tc, tc), 0)
    ci = lax.broadcasted_iota(jnp.int32, (tc, tc), 1)
    causal = ri >= ci
    eye = ri == ci
    m_lane = lax.broadcasted_iota(jnp.int32, (SUBLANES, 1, LANES), 2)
    kscale = DH_B ** -0.5

    gt = _get_batches(g_s, tc, 0, LANES)
    m_old = m1_ref[...]
    m_new = m_old
    for h in range(H_B):
        cols = slice(h * DH_B, (h + 1) * DH_B)
        y_a = _dot(ga_ref[...], wa_ref[:, cols])

        ig = gt[:, :, h:h + 1]
        bc = gt[:, :, H_B + h:H_B + h + 1]
        a_row = jnp.sum(jnp.where(eye, ig - bc, 0.0), axis=1, keepdims=True)
        dlog = jnp.where(causal, bc + a_row, -jnp.inf)
        inter = bc + m_old[:, :, h:h + 1]
        m = jnp.maximum(inter, jnp.max(dlog, axis=2, keepdims=True))
        p = jnp.exp(dlog - m)
        w_inter = jnp.exp(inter - m)
        floor = jnp.exp(-m)
        m_end = m[:, tc - 1:tc, :]
        w_end = jnp.exp(bc[:, tc - 1:tc, :] - bc + ig - m_end)
        decay = jnp.exp(inter[:, tc - 1:tc, :] - m_end)
        n0 = n1_ref[:, :, cols]
        q = _get_batches(q_s, tc, h * DH_B, DH_B)
        k = _get_batches(k_s, tc, h * DH_B, DH_B) * kscale
        v_bf = _get_batches(v_s, tc, h * DH_B, DH_B).astype(BF16)
        q_bf = q.astype(BF16)
        sc = _bdot_nt(q_bf, k.astype(BF16)) * p
        c0 = c_s[h]
        num = _bdot(sc.astype(BF16), v_bf) + w_inter * _bdot(q_bf, c0.astype(BF16))
        den = jnp.sum(sc, axis=2, keepdims=True) + w_inter * jnp.sum(q * n0, axis=2, keepdims=True)
        hv = num * (1.0 / jnp.maximum(jnp.abs(den), floor))
        y_b = _dot(gb_s[...], wb_ref[:, cols])
        _put_batches(h_s, tc, h * DH_B, _head_norm(hv))
        kw = k * w_end
        c_s[h] = decay * c0 + _bdot_tn(kw.astype(BF16), v_bf)
        n1_ref[:, :, cols] = jnp.where(live, decay * n0 + jnp.sum(kw, axis=1, keepdims=True), n0)
        m_new = jnp.where(m_lane == h, m_end, m_new)
        res = jnp.concatenate([xs[j] for j in range(h * DH_B // LANES, (h + 1) * DH_B // LANES)], axis=-1)
        h1_ref[:, cols] = res + y_a + y_b
    m1_ref[...] = jnp.where(live, m_new, m_old)

    y = _get_all(h_s) * nw_ref[...] + skip_ref[...] * xc_s[...]
    gb_s[...] = (y * _silu(zb_ref[...])).astype(BF16)


def mlstm_mixer(x, ga, u3, buf0, c0, n0, m0, w, tc, g_off):
    nbg, rows, _ = u3.shape
    rc = tc * SUBLANES
    nch = rows // rc
    grp = lambda shape: pl.BlockSpec((None,) + shape, lambda g, c: (g, 0, 0))
    per_b = lambda width: pl.BlockSpec((SUBLANES, 1, width), lambda g, c: (g, 0, 0))
    cur = lambda blk: (lambda g, c: (g, jnp.minimum(c, nch - 1), blk))
    prev = lambda g, c: (g, jnp.maximum(c - 1, 0), 0)
    blocked = pltpu.VMEM((W_B // LANES, rc, LANES), F32)
    n_given = c0.shape[0] // SUBLANES - g_off
    return pl.pallas_call(
        functools.partial(_mlstm_kernel, tc=tc, n_given=n_given, g_off=g_off),
        grid=(nbg, nch + 1),
        in_specs=[pl.BlockSpec((SUBLANES, tc, D_MODEL), prev),
                  pl.BlockSpec((None, rc, W_A), prev),
                  pl.BlockSpec((None, rc, W_B), cur(1)),
                  pl.BlockSpec((None, rc, W_B), cur(2)),
                  grp((HIST, W_B)),
                  pl.BlockSpec(memory_space=pl.ANY),
                  per_b(W_B), per_b(LANES),
                  _const_spec((CONV_K, W_B)), _const_spec((1, W_B)),
                  _const_spec((H_B, DH_B, DH_B)), _const_spec((H_B, DH_B, DH_B)), _const_spec((H_B, DH_B, DH_B)),
                  _const_spec((3 * W_B, LANES)), _const_spec((1, LANES)),
                  _const_spec((1, W_B)), _const_spec((1, W_B)),
                  _const_spec((W_A, D_MODEL)), _const_spec((W_B, D_MODEL))],
        out_specs=[pl.BlockSpec((None, rc, D_MODEL), prev),
                   pl.BlockSpec(memory_space=pl.ANY),
                   per_b(W_B), per_b(LANES), grp((HIST, W_B))],
        out_shape=[jax.ShapeDtypeStruct((nbg, rows, D_MODEL), F32),
                   jax.ShapeDtypeStruct((nbg * SUBLANES,) + c0.shape[1:], F32),
                   jax.ShapeDtypeStruct((nbg * SUBLANES, 1, W_B), F32),
                   jax.ShapeDtypeStruct((nbg * SUBLANES, 1, LANES), F32),
                   jax.ShapeDtypeStruct((nbg, HIST, W_B), F32)],
        scratch_shapes=[pltpu.VMEM((rc + HIST, W_B), F32),
                        pltpu.VMEM((rc, W_B), F32), blocked, blocked, blocked,
                        pltpu.VMEM((1, rc, LANES), F32), blocked,
                        pltpu.VMEM((H_B, SUBLANES, DH_B, DH_B), F32),
                        pltpu.VMEM((rc, W_B), BF16), blocked,
                        pltpu.SemaphoreType.DMA((H_B,))],
        compiler_params=_cparams(("parallel", "arbitrary")),
        name="mlstm_mixer",
    )(x, ga, u3, u3, buf0, c0, n0, m0, w["ml_cw"], w["ml_cb"], w["ml_wq"], w["ml_wk"], w["ml_wv"],
      w["ml_wif"], w["ml_bif"], w["ml_nw"], w["ml_skip"], w["ev_w_out_a"], w["ev_w_out_b"])


def _ret_log_decay(h):
    return float(np.log1p(-np.exp2(-5.0 - h)))


OD_MIX_IN = 2 * QK_C + 2 * W_C


def _retention_kernel(h_ref, g1_ref, w_ref, cos_ref, sin_ref, s0_hbm, nw_ref,
                      gc_ref, s1_hbm, ud_ref, hn_s, slot, zc_s, q_s, k_s, v_s, o_s, st_s, sem, *, tc, n_given, g_off):
    g = pl.program_id(0)
    c = pl.program_id(1)
    started = c > 0

    @pl.when(c == 0)
    def _():
        slot[...] = jnp.zeros(slot.shape, F32)

    def state_copy(h, to_hbm):
        hbm = (s1_hbm.at[pl.ds(g * SUBLANES, SUBLANES), h] if to_hbm else
               s0_hbm.at[pl.ds((g + g_off) * SUBLANES, SUBLANES), h])
        src, dst = (st_s.at[h], hbm) if to_hbm else (hbm, st_s.at[h])
        return pltpu.make_async_copy(src, dst, sem.at[h])

    @pl.when((c == 0) & (g < n_given))
    def _():
        for h in range(H_C):
            state_copy(h, False).start()
        for h in range(H_C):
            state_copy(h, False).wait()

    @pl.when((c == 0) & (g >= n_given))
    def _():
        st_s[...] = jnp.zeros(st_s.shape, F32)

    hn_s[...] = _rms(h_ref[...], g1_ref[...]).astype(BF16)

    def project(j):
        cols = slice(j * MXU_DIM, (j + 1) * MXU_DIM)
        r = _dot(hn_s[...], w_ref[:, cols])
        if (j + 1) * MXU_DIM <= OD_MIX_IN:
            slot[:, cols] = r
        else:
            ud_ref[:, j * MXU_DIM - OD_MIX_IN:(j + 1) * MXU_DIM - OD_MIX_IN] = r

    lru_blocks = iter(range(OD_MIX_IN // MXU_DIM, OD_IN // MXU_DIM))
    per_copy = (OD_IN - OD_MIX_IN) // MXU_DIM // H_C
    cos, sin = cos_ref[...], sin_ref[...]
    _put_cols(v_s, 0, slot[:, 2 * QK_C:2 * QK_C + W_C])
    zc_s[...] = slot[:, 2 * QK_C + W_C:OD_MIX_IN]
    for h in range(H_C):
        for _ in range(per_copy):
            project(next(lru_blocks))
        for off, dst, scale in ((0, q_s, None), (QK_C, k_s, DK_C ** -0.5)):
            x = slot[:, off + h * DK_C:off + (h + 1) * DK_C]
            r = x * cos + pltpu.roll(x, DK_C // 2, 1) * sin
            dst[h] = r if scale is None else r * scale

    per_head = OD_MIX_IN // MXU_DIM // H_C

    ri = lax.broadcasted_iota(jnp.int32, (tc, tc), 0)
    ci = lax.broadcasted_iota(jnp.int32, (tc, tc), 1)
    diff = (ri - ci).astype(F32)
    tcol = lax.broadcasted_iota(jnp.int32, (tc, 1), 0).astype(F32)
    dmask, xi, zeta, gfull = [], [], [], []
    for h in range(H_C):
        lg = _ret_log_decay(h)
        dmask.append(jnp.where(diff >= 0, jnp.exp(lg * jnp.maximum(diff, 0.0)), 0.0))
        xi.append(jnp.exp(lg * (tcol + 1.0)))
        zeta.append(jnp.exp(lg * (tc - 1.0 - tcol)))
        gfull.append(float(np.exp(np.float32(lg) * np.float32(tc))))

    for h in range(H_C):
        nxt = iter(range(h * per_head, (h + 1) * per_head))
        q_bf = _get_batches(q_s, tc, h * DK_C, DK_C).astype(BF16)
        k = _get_batches(k_s, tc, h * DK_C, DK_C)
        v_bf = _get_batches(v_s, tc, h * DV_C, DV_C).astype(BF16)
        s0 = st_s[h]
        inner = _bdot_nt(q_bf, k.astype(BF16)) * dmask[h]
        project(next(nxt))
        o = _bdot(inner.astype(BF16), v_bf) + xi[h] * _bdot(q_bf, s0.astype(BF16))
        project(next(nxt))
        _put_batches(o_s, tc, h * DV_C, _head_norm(o))
        decay = jnp.where(started, gfull[h], 1.0)
        st_s[h] = decay * s0 + _bdot_tn((k * zeta[h]).astype(BF16), v_bf)
        for j in nxt:
            project(j)

    gc_ref[...] = (_get_all(o_s) * nw_ref[...] * _silu(zc_s[...])).astype(BF16)

    @pl.when(c == pl.num_programs(1) - 1)
    def _():
        for h in range(H_C):
            state_copy(h, True).start()
        for h in range(H_C):
            state_copy(h, True).wait()


def retention_mixer(h1, cos, sin, s0, w, tc, g_off):
    nbg, rows, d = h1.shape
    rc = tc * SUBLANES
    nch = rows // rc
    assert OD_MIX_IN % (MXU_DIM * H_C) == 0 and (OD_IN - OD_MIX_IN) % (MXU_DIM * H_C) == 0
    n_given = s0.shape[0] // SUBLANES - g_off
    cur = lambda g, c: (g, jnp.minimum(c, nch - 1), 0)
    prev = lambda g, c: (g, jnp.maximum(c - 1, 0), 0)
    return pl.pallas_call(
        functools.partial(_retention_kernel, tc=tc, n_given=n_given, g_off=g_off),
        grid=(nbg, nch + 1),
        in_specs=[pl.BlockSpec((None, rc, d), cur),
                  _const_spec((1, d)), _const_spec((d, OD_IN)),
                  pl.BlockSpec((None, rc, DK_C), prev),
                  pl.BlockSpec((None, rc, DK_C), prev),
                  pl.BlockSpec(memory_space=pl.ANY),
                  _const_spec((1, W_C))],
        out_specs=[pl.BlockSpec((None, rc, W_C), prev),
                   pl.BlockSpec(memory_space=pl.ANY),
                   pl.BlockSpec((None, rc, OD_IN - OD_MIX_IN), cur)],
        out_shape=[jax.ShapeDtypeStruct((nbg, rows, W_C), BF16),
                   jax.ShapeDtypeStruct((nbg * SUBLANES,) + s0.shape[1:], F32),
                   jax.ShapeDtypeStruct((nbg, rows, OD_IN - OD_MIX_IN), F32)],
        scratch_shapes=[pltpu.VMEM((rc, d), BF16), pltpu.VMEM((rc, OD_MIX_IN), F32), pltpu.VMEM((rc, W_C), F32),
                        pltpu.VMEM((QK_C // LANES, rc, LANES), F32), pltpu.VMEM((QK_C // LANES, rc, LANES), F32),
                        pltpu.VMEM((W_C // LANES, rc, LANES), F32), pltpu.VMEM((W_C // LANES, rc, LANES), F32),
                        pltpu.VMEM((H_C, SUBLANES, DK_C, DV_C), F32),
                        pltpu.SemaphoreType.DMA((H_C,))],
        compiler_params=_cparams(("parallel", "arbitrary")),
        name="retention_mixer",
    )(h1, w["norm1"], w["od_w_in"], cos, sin, s0, w["ret_nw"])


def _rglru_kernel(res_ref, gc_ref, xd_ref, zd_ref, buf0_ref, h0_ref, cw_ref, cb_ref, wa_ref, ba_ref, wx_ref,
                  bx_ref, lam_ref, woc_ref, wod_ref, fn_ref,
                  y_ref, h1_ref, buf1_ref, xpad, a_s, b_s, gd_s, ys, inv_s, *, tc):
    c = pl.program_id(1)
    nch = pl.num_programs(1) - 1
    live = c < nch
    rc = tc * SUBLANES

    @pl.when(c == 0)
    def _():
        h1_ref[...] = h0_ref[...]
        gd_s[...] = jnp.zeros(gd_s.shape, BF16)

    _conv_history(c, buf0_ref, xpad, buf1_ref, rc)

    xpad[HIST:HIST + rc, :] = xd_ref[...]
    buf1_ref[...] = jnp.where(live, xpad[rc:rc + HIST, :], buf1_ref[...])
    nsp = (-0.5 * LRU_C) * _softplus(-lam_ref[...])
    ssq = jnp.zeros((rc, 1), F32)
    for j in range(W_D // MXU_DIM):
        cols = slice(j * MXU_DIM, (j + 1) * MXU_DIM)
        y_c = _dot(gc_ref[...], woc_ref[:, cols])
        xc = cb_ref[:, cols]
        for tap in range(CONV_K):
            xc = xc + xpad[tap * SUBLANES:tap * SUBLANES + rc, cols] * cw_ref[tap:tap + 1, cols]
        lhs = xc.astype(BF16)
        t_r = jnp.tanh(0.5 * (_dot(lhs, wa_ref[j]) + ba_ref[:, cols]))
        y_d = _dot(gd_s[...], wod_ref[:, cols])
        i = _sigmoid(_dot(lhs, wx_ref[j]) + bx_ref[:, cols])
        log_a = nsp[:, cols] * t_r + nsp[:, cols]
        a = jnp.exp(log_a)
        a_s[:, cols] = a
        b_s[:, cols] = jnp.sqrt(-jnp.tanh(log_a) * (a * a + 1.0)) * (i * xc)
        yj = res_ref[:, cols] + y_c + y_d
        _put_cols(ys, j * MXU_DIM, yj)
        ssq = ssq + jnp.sum(yj * yj, axis=-1, keepdims=True)

    inv_s[0] = jnp.broadcast_to(lax.rsqrt(ssq * (1.0 / D_MODEL) + EPS), (rc, LANES))
    for b in range(SUBLANES):
        inv_b = _get_batch(inv_s, b, tc, 0, LANES)[:, 0:1]
        y_ref[b] = _get_batch(ys, b, tc, 0, D_MODEL) * inv_b * fn_ref[...]

    def step(t, h):
        r0 = pl.multiple_of(t * SUBLANES, SUBLANES)
        h = a_s[pl.ds(r0, SUBLANES), :] * h + b_s[pl.ds(r0, SUBLANES), :]
        b_s[pl.ds(r0, SUBLANES), :] = h
        return h

    h_old = h1_ref[...]
    h1_ref[...] = jnp.where(live, lax.fori_loop(0, tc, step, h_old, unroll=4), h_old)
    gd_s[...] = (b_s[...] * _silu(zd_ref[...])).astype(BF16)


def rglru_mixer(res, gc, u3, buf0, h0, w, tc):
    nbg, rows, _ = u3.shape
    rc = tc * SUBLANES
    nch = rows // rc
    nblk = W_D // MXU_DIM
    grp = lambda shape: pl.BlockSpec((None,) + shape, lambda g, c: (g, 0, 0))
    cur = lambda blk: (lambda g, c: (g, jnp.minimum(c, nch - 1), blk))
    prev = lambda g, c: (g, jnp.maximum(c - 1, 0), 0)
    return pl.pallas_call(
        functools.partial(_rglru_kernel, tc=tc),
        grid=(nbg, nch + 1),
        in_specs=[pl.BlockSpec((None, rc, D_MODEL), prev),
                  pl.BlockSpec((None, rc, W_C), prev),
                  pl.BlockSpec((None, rc, W_D), cur(0)),
                  pl.BlockSpec((None, rc, W_D), cur(1)),
                  grp((HIST, W_D)), grp((SUBLANES, W_D)),
                  _const_spec((CONV_K, W_D)), _const_spec((1, W_D)),
                  _const_spec((nblk, MXU_DIM, MXU_DIM)), _const_spec((1, W_D)),
                  _const_spec((nblk, MXU_DIM, MXU_DIM)), _const_spec((1, W_D)),
                  _const_spec((1, W_D)),
                  _const_spec((W_C, D_MODEL)), _const_spec((W_D, D_MODEL)), _const_spec((1, D_MODEL))],
        out_specs=[pl.BlockSpec((SUBLANES, tc, D_MODEL), prev),
                   grp((SUBLANES, W_D)), grp((HIST, W_D))],
        out_shape=[jax.ShapeDtypeStruct((nbg * SUBLANES, rows // SUBLANES, D_MODEL), F32),
                   jax.ShapeDtypeStruct((nbg, SUBLANES, W_D), F32),
                   jax.ShapeDtypeStruct((nbg, HIST, W_D), F32)],
        scratch_shapes=[pltpu.VMEM((rc + HIST, W_D), F32), pltpu.VMEM((rc, W_D), F32), pltpu.VMEM((rc, W_D), F32),
                        pltpu.VMEM((rc, W_D), BF16), pltpu.VMEM((D_MODEL // LANES, rc, LANES), F32),
                        pltpu.VMEM((1, rc, LANES), F32)],
        compiler_params=_cparams(("parallel", "arbitrary")),
        name="rglru_mixer",
    )(res, gc, u3, u3, buf0, h0, w["lru_cw"], w["lru_cb"], w["lru_wa"], w["lru_ba"], w["lru_wx"], w["lru_bx"],
      w["lru_lam"], w["od_w_out_c"], w["od_w_out_d"], w["final_norm"])


def _to_tm(x):
    b, t, c = x.shape
    return x.reshape(b // SUBLANES, SUBLANES, t, c).swapaxes(1, 2).reshape(b // SUBLANES, t * SUBLANES, c)


def _from_tm(x, t):
    nbg, _, c = x.shape
    return x.reshape(nbg, t, SUBLANES, c).swapaxes(1, 2).reshape(nbg * SUBLANES, t, c)


def _rope_tables(pos):
    half = DK_C // 2
    inv = ROPE_BASE ** (-jnp.arange(half, dtype=F32) / half)
    ang = pos[..., None] * inv
    cos = jnp.cos(ang)
    sin = jnp.sin(ang)
    cos = jnp.concatenate([cos, cos], axis=-1)
    sin = jnp.concatenate([-sin, sin], axis=-1)
    rep = lambda a: jnp.repeat(a, SUBLANES, axis=1)
    return rep(cos), rep(sin)


def _prep_weights(p):
    w = {}
    w["norm0"] = p["norm_w"][0][None]
    w["norm1"] = p["norm_w"][1][None]
    w["final_norm"] = p["final_norm_w"][None]
    w["ev_w_in"] = p["ev_w_in"][0].astype(BF16)
    w["ev_w_out_a"] = p["ev_w_out"][0][:W_A].astype(BF16)
    w["ev_w_out_b"] = p["ev_w_out"][0][W_A:].astype(BF16)
    are, aim, bbre, bbim = s5_prep(p["s5_lambda_re"][0], p["s5_lambda_im"][0], p["s5_log_dt"][0],
                                   p["s5_b_re"][0], p["s5_b_im"][0])
    per = MXU_DIM // S5_GROUP
    w["s5_are"] = are.reshape(1, W_S)
    w["s5_aim"] = aim.reshape(1, W_S)
    w["s5_wbre"] = _block_diag(bbre, per).astype(BF16)
    w["s5_wbim"] = _block_diag(bbim, per).astype(BF16)
    w["s5_wcre"] = _block_diag(jnp.swapaxes(p["s5_c_re"][0], 1, 2), per).astype(BF16)
    w["s5_wcim"] = _block_diag(-jnp.swapaxes(p["s5_c_im"][0], 1, 2), per).astype(BF16)
    w["s5_d"] = p["s5_d"][0][None]
    w["s5_wglu"] = p["s5_w_glu"][0].astype(BF16)
    w["s5_bglu"] = p["s5_b_glu"][0][None]
    w["ml_cw"] = p["ml_conv_w"][0]
    w["ml_cb"] = p["ml_conv_b"][0][None]
    w["ml_wq"] = p["ml_wq"][0].astype(BF16)
    w["ml_wk"] = p["ml_wk"][0].astype(BF16)
    w["ml_wv"] = p["ml_wv"][0].astype(BF16)
    w["ml_wif"] = jnp.pad(p["ml_w_if"][0], ((0, 0), (0, LANES - 2 * H_B))).astype(BF16)
    w["ml_bif"] = jnp.pad(p["ml_b_if"][0], (0, LANES - 2 * H_B))[None]
    w["ml_nw"] = p["ml_norm_w"][0][None]
    w["ml_skip"] = p["ml_skip"][0][None]
    w["od_w_in"] = p["od_w_in"][0].astype(BF16)
    w["od_w_out_c"] = p["od_w_out"][0][:W_C].astype(BF16)
    w["od_w_out_d"] = p["od_w_out"][0][W_C:].astype(BF16)
    w["ret_nw"] = p["ret_norm_w"][0][None]
    w["lru_cw"] = p["lru_conv_w"][0]
    w["lru_cb"] = p["lru_conv_b"][0][None]
    w["lru_wa"] = _block_diag(p["lru_w_a"][0], MXU_DIM // BD_D).astype(BF16)
    w["lru_ba"] = p["lru_b_a"][0][None]
    w["lru_wx"] = _block_diag(p["lru_w_x"][0], MXU_DIM // BD_D).astype(BF16)
    w["lru_bx"] = p["lru_b_x"][0][None]
    w["lru_lam"] = p["lru_lambda"][0][None]
    return w


def _conv_to_tm(buf):
    return _to_tm(buf)


def _trunk(x, pos, st, w, tc, g_off=0):
    new = {}
    u = proj_in(x, w["norm0"], w["ev_w_in"], True)
    ga, new["s5_re"], new["s5_im"] = s5_mixer(u, st["s5_re"], st["s5_im"], w, tc)
    h1, new["ml_c"], new["ml_n"], new["ml_m"], new["ml_conv"] = mlstm_mixer(
        x, ga, u, st["ml_conv"], st["ml_c"], st["ml_n"], st["ml_m"], w, tc, g_off)
    cos, sin = _rope_tables(pos)
    gc, new["ret"], ud = retention_mixer(h1, cos, sin, st["ret"], w, tc, g_off)
    y, new["lru_h"], new["lru_conv"] = rglru_mixer(h1, gc, ud, st["lru_conv"], st["lru_h"], w, tc)
    return y, new


def _states_in(s5_re, s5_im, ml_c, ml_n, ml_m, ml_conv, ret, lru_h, lru_conv):
    b = s5_re.shape[0]
    nbg = b // SUBLANES
    return dict(
        s5_re=s5_re.reshape(nbg, SUBLANES, W_S),
        s5_im=s5_im.reshape(nbg, SUBLANES, W_S),
        ml_c=ml_c,
        ml_n=ml_n.reshape(b, 1, W_B),
        ml_m=jnp.pad(ml_m, ((0, 0), (0, LANES - H_B))).reshape(b, 1, LANES),
        ml_conv=_conv_to_tm(ml_conv),
        ret=ret,
        lru_h=lru_h.reshape(nbg, SUBLANES, W_D),
        lru_conv=_conv_to_tm(lru_conv),
    )


def _states_out(st, sel=None):
    nbg = st["s5_re"].shape[0]
    b = nbg * SUBLANES
    out = dict(
        s5_re=st["s5_re"].reshape(b, G_A, S5_STATE),
        s5_im=st["s5_im"].reshape(b, G_A, S5_STATE),
        ml_c=st["ml_c"].reshape(b, H_B, DH_B, DH_B),
        ml_n=st["ml_n"].reshape(b, H_B, DH_B),
        ml_m=st["ml_m"].reshape(b, LANES)[:, :H_B],
        ml_conv=_from_tm(st["ml_conv"], CONV_K - 1),
        ret=st["ret"].reshape(b, H_C, DK_C, DV_C),
        lru_h=st["lru_h"].reshape(b, W_D),
        lru_conv=_from_tm(st["lru_conv"], CONV_K - 1),
    )
    if sel is not None:
        out = {k: v[sel] for k, v in out.items()}
    return out


_STATE_ORDER = ("s5_re", "s5_im", "ml_c", "ml_n", "ml_m", "ml_conv", "ret", "lru_h", "lru_conv")


def kernel(x_prompt, x_sample, state_s5_re, state_s5_im, state_ml_c, state_ml_n, state_ml_m, state_ml_conv, state_ret, state_lru_h, state_lru_conv, meta, norm_w, final_norm_w, ev_w_in, ev_w_out, s5_lambda_re, s5_lambda_im, s5_log_dt, s5_b_re, s5_b_im, s5_c_re, s5_c_im, s5_d, s5_w_glu, s5_b_glu, ml_conv_w, ml_conv_b, ml_wq, ml_wk, ml_wv, ml_w_if, ml_b_if, ml_norm_w, ml_skip, od_w_in, od_w_out, ret_norm_w, lru_conv_w, lru_conv_b, lru_w_a, lru_b_a, lru_w_x, lru_b_x, lru_lambda):
    p = dict(norm_w=norm_w, final_norm_w=final_norm_w, ev_w_in=ev_w_in, ev_w_out=ev_w_out,
             s5_lambda_re=s5_lambda_re, s5_lambda_im=s5_lambda_im, s5_log_dt=s5_log_dt,
             s5_b_re=s5_b_re, s5_b_im=s5_b_im, s5_c_re=s5_c_re, s5_c_im=s5_c_im, s5_d=s5_d,
             s5_w_glu=s5_w_glu, s5_b_glu=s5_b_glu, ml_conv_w=ml_conv_w, ml_conv_b=ml_conv_b,
             ml_wq=ml_wq, ml_wk=ml_wk, ml_wv=ml_wv, ml_w_if=ml_w_if, ml_b_if=ml_b_if,
             ml_norm_w=ml_norm_w, ml_skip=ml_skip, od_w_in=od_w_in, od_w_out=od_w_out,
             ret_norm_w=ret_norm_w, lru_conv_w=lru_conv_w, lru_conv_b=lru_conv_b,
             lru_w_a=lru_w_a, lru_b_a=lru_b_a, lru_w_x=lru_w_x, lru_b_x=lru_b_x, lru_lambda=lru_lambda)
    assert ev_w_in.shape[0] == 1 and od_w_in.shape[0] == 1, "two-layer trunk"
    w = _prep_weights(p)
    bp, tp, _ = x_prompt.shape
    bs, ts, _ = x_sample.shape
    assert ts == N_META and bp % SUBLANES == 0 and bs % SUBLANES == 0 and tp % CHUNK == 0

    meta_b = jnp.broadcast_to(meta[None], (bp, N_META, D_MODEL))
    x_short = jnp.concatenate([x_sample, meta_b], axis=0)
    given = dict(s5_re=state_s5_re[0], s5_im=state_s5_im[0], ml_c=state_ml_c[0], ml_n=state_ml_n[0],
                 ml_m=state_ml_m[0], ml_conv=state_ml_conv[0], ret=state_ret[0], lru_h=state_lru_h[0],
                 lru_conv=state_lru_conv[0])
    both = {k: s if k in ("ml_c", "ret") else jnp.concatenate([s, jnp.zeros((bp,) + s.shape[1:], s.dtype)], axis=0)
            for k, s in given.items()}
    tpos = jnp.arange(N_META, dtype=F32)
    pos_short = jnp.concatenate([jnp.broadcast_to((N_META + PAST_LEN) + tpos, (bs // SUBLANES, N_META)),
                                 jnp.broadcast_to(tpos, (bp // SUBLANES, N_META))], axis=0)
    y_short, st_short = _trunk(x_short, pos_short, _states_in(**both), w, N_META)
    y_sample = y_short[:bs]
    new_s = _states_out(st_short, slice(0, bs))

    nbs = bs // SUBLANES
    nbg_short = (bs + bp) // SUBLANES
    st_meta = {}
    for name, val in st_short.items():
        per_group = val.shape[0] // nbg_short
        st_meta[name] = val if name in ("ml_c", "ret") else val[nbs * per_group:]
    pos_long = jnp.broadcast_to(N_META + jnp.arange(tp, dtype=F32), (bp // SUBLANES, tp))
    y_prompt, st_long = _trunk(x_prompt, pos_long, st_meta, w, CHUNK, nbs)
    new_p = _states_out(st_long)

    return ((y_prompt, y_sample)
            + tuple(new_p[k][None] for k in _STATE_ORDER)
            + tuple(new_s[k][None] for k in _STATE_ORDER))
```
